```python
import math
import jax
import jax.numpy as jnp
from jax import lax
import numpy as np

D_MODEL = 1024
BATCH = 2
SEQ = 8192
DEPTH = 1

GRID_W = 64
CTX_LEN = 256
NORM_EPS = 1e-6
DN_HEADS = 8
DN_DK = 128
DN_DV = 128
DN_WIDTH = DN_HEADS * DN_DV
CONV_K = 5
CHUNK = 64
S5_WIDTH = 512
S5_GROUP = 16
S5_GROUPS = S5_WIDTH // S5_GROUP
S5_STATE = 64
QK_W = DN_HEADS * DN_DK
QKV_W = 2 * QK_W + DN_WIDTH
O_BETA = QKV_W
O_DECAY = O_BETA + 2 * DN_HEADS
O_U = O_DECAY + 2 * DN_HEADS
STATE_COLS = O_U + S5_WIDTH
O_ZDN = STATE_COLS
O_ZS5 = O_ZDN + DN_WIDTH
O_GDN = O_ZS5 + S5_WIDTH
O_GS5 = O_GDN + D_MODEL
IN_WIDTH = O_GS5 + D_MODEL

kernel_name = 'hybrid_deltanet_s5_prefix_ctx_block'


def rms_norm(x, w):
    xf = x.astype(jnp.float32)
    y = xf * lax.rsqrt(jnp.mean(xf * xf, axis=-1, keepdims=True) + NORM_EPS)
    return (y * w.astype(jnp.float32)).astype(x.dtype)


def l2_normalise(t):
    return t * lax.rsqrt(jnp.sum(t * t, axis=-1, keepdims=True) + NORM_EPS)


def short_conv(u, w, rows):
    bsz, length, ch = u.shape
    if rows is not None:
        u = u.reshape(bsz * rows, GRID_W, ch)
    y = lax.conv_general_dilated(u, w[:, None, :].astype(u.dtype), window_strides=(1,),
                                 padding=[(CONV_K // 2, CONV_K // 2)],
                                 dimension_numbers=('NWC', 'WIO', 'NWC'),
                                 feature_group_count=ch)
    return jax.nn.silu(y.reshape(bsz, length, ch))


def gated_delta_chunked(q, k, v, g, beta, s0, need_out):
    bsz, nh, length, dk = q.shape
    dv = v.shape[-1]
    n = length // CHUNK

    def blocks(t):
        return t.reshape(bsz, nh, n, CHUNK, *t.shape[3:])

    q, k, v, g, beta = blocks(q), blocks(k), blocks(v), blocks(g), blocks(beta)
    G = jnp.cumsum(g, axis=-1)
    pos = jnp.arange(CHUNK)
    lower_incl = pos[:, None] >= pos[None, :]
    lower_strict = pos[:, None] > pos[None, :]
    decay = jnp.exp(jnp.where(lower_incl, G[..., :, None] - G[..., None, :], -jnp.inf))
    kb = k * beta[..., None]
    t_mat = jnp.where(lower_strict, jnp.einsum('bhncd,bhnsd->bhncs', kb, k) * decay, 0.0) \
        + jnp.eye(CHUNK, dtype=jnp.float32)
    u = lax.linalg.triangular_solve(t_mat, v * beta[..., None], left_side=True, lower=True, unit_diagonal=True)
    w = lax.linalg.triangular_solve(t_mat, kb * jnp.exp(G)[..., None], left_side=True, lower=True,
                                    unit_diagonal=True)
    k_end = k * jnp.exp(G[..., -1:] - G)[..., None]
    a_end = jnp.exp(G[..., -1])
    if s0 is None:
        s0 = jnp.zeros((bsz, nh, dk, dv), jnp.float32)
    front = lambda t: jnp.moveaxis(t, 2, 0)
    if need_out:
        attn = jnp.einsum('bhncd,bhnsd->bhncs', q, k) * decay
        q_dec = q * jnp.exp(G)[..., None]
        xs = tuple(map(front, (u, w, k_end, a_end, q_dec, attn)))
    else:
        xs = tuple(map(front, (u, w, k_end, a_end)))

    def step(S, xs_c):
        u_c, w_c, k_c, a_c = xs_c[:4]
        v_new = u_c - jnp.einsum('bhcd,bhdv->bhcv', w_c, S)
        S_next = S * a_c[..., None, None] + jnp.einsum('bhcd,bhcv->bhdv', k_c, v_new)
        if need_out:
            qd_c, at_c = xs_c[4], xs_c[5]
            o = jnp.einsum('bhcd,bhdv->bhcv', qd_c, S) + jnp.einsum('bhcs,bhsv->bhcv', at_c, v_new)
            return S_next, o
        return S_next, None

    S, o = lax.scan(step, s0, xs)
    if need_out:
        o = jnp.moveaxis(o, 0, 2).reshape(bsz, nh, length, dv)
    return o, S


def s5_discretise(lam_re, lam_im, log_step, b_re, b_im):
    lam_re = lam_re.astype(jnp.float32)
    lam_im = lam_im.astype(jnp.float32)
    dt = jnp.exp(log_step.astype(jnp.float32))[:, None]
    mag = jnp.exp(lam_re * dt)
    abar_re = mag * jnp.cos(lam_im * dt)
    abar_im = mag * jnp.sin(lam_im * dt)
    num_re = abar_re - 1.0
    num_im = abar_im
    den = lam_re * lam_re + lam_im * lam_im
    f_re = (num_re * lam_re + num_im * lam_im) / den
    f_im = (num_im * lam_re - num_re * lam_im) / den
    b_re = b_re.astype(jnp.float32)
    b_im = b_im.astype(jnp.float32)
    bbar_re = f_re[..., None] * b_re - f_im[..., None] * b_im
    bbar_im = f_re[..., None] * b_im + f_im[..., None] * b_re
    return abar_re, abar_im, bbar_re, bbar_im


def s5_scan(u, abar_re, abar_im, bbar_re, bbar_im, h0, reverse):
    bu_re = jnp.einsum('blgc,gpc->blgp', u, bbar_re)
    bu_im = jnp.einsum('blgc,gpc->blgp', u, bbar_im)
    a_re = jnp.broadcast_to(abar_re, bu_re.shape)
    a_im = jnp.broadcast_to(abar_im, bu_re.shape)

    def combine(e1, e2):
        a1r, a1i, b1r, b1i = e1
        a2r, a2i, b2r, b2i = e2
        return (a1r * a2r - a1i * a2i, a1r * a2i + a1i * a2r,
                a2r * b1r - a2i * b1i + b2r, a2r * b1i + a2i * b1r + b2i)

    ar, ai, hr, hi = lax.associative_scan(combine, (a_re, a_im, bu_re, bu_im), reverse=reverse, axis=1)
    if h0 is not None:
        h0r, h0i = h0[0][:, None], h0[1][:, None]
        hr = hr + ar * h0r - ai * h0i
        hi = hi + ar * h0i + ai * h0r
    return hr, hi


def mixer(h, lp, init, rows, need_out):
    bsz, length, _ = h.shape
    f32 = jnp.float32
    w_in = lp['w_in'] if need_out else lp['w_in'][:, :STATE_COLS]
    p = jnp.einsum('bld,de->ble', h, w_in)

    qkv = short_conv(p[..., :QKV_W], lp['conv_w'], rows)

    def heads(t, dh):
        return t.reshape(bsz, length, DN_HEADS, dh).transpose(0, 2, 1, 3).astype(f32)

    q = l2_normalise(heads(qkv[..., :QK_W], DN_DK)) * (DN_DK ** -0.5)
    k = l2_normalise(heads(qkv[..., QK_W:2 * QK_W], DN_DK))
    v = heads(qkv[..., 2 * QK_W:], DN_DV)
    beta = jax.nn.sigmoid(p[..., O_BETA:O_DECAY].astype(f32)).reshape(bsz, length, 2, DN_HEADS)
    beta = beta.transpose(2, 0, 3, 1)
    dec = p[..., O_DECAY:O_U].astype(f32).reshape(bsz, length, 2, DN_HEADS)
    g = -jnp.exp(lp['dn_A_log'].astype(f32)) * jax.nn.softplus(dec + lp['dn_dt_bias'].astype(f32))
    g = g.transpose(2, 0, 3, 1)
    flip = lambda t: jnp.flip(t, axis=2)
    o_f, s_f = gated_delta_chunked(q, k, v, g[0], beta[0], None if init is None else init[0], need_out)
    o_b, s_b = gated_delta_chunked(flip(q), flip(k), flip(v), flip(g[1]), flip(beta[1]),
                                   None if init is None else init[1], need_out)

    u = p[..., O_U:STATE_COLS].astype(f32).reshape(bsz, length, S5_GROUPS, S5_GROUP)
    ssm = []
    finals = []
    for d in range(2):
        reverse = d == 1
        abr, abi, bbr, bbi = s5_discretise(lp['s5_lam_re'][d], lp['s5_lam_im'][d], lp['s5_log_step'][d],
                                           lp['s5_B_re'][d], lp['s5_B_im'][d])
        hr, hi = s5_scan(u, abr, abi, bbr, bbi, None if init is None else init[2 + d], reverse)
        end = 0 if reverse else -1
        finals.append((hr[:, end], hi[:, end]))
        ssm.append((hr, hi))
    states = (s_f, s_b, finals[0], finals[1])
    if not need_out:
        return None, states

    o = (o_f + flip(o_b)).transpose(0, 2, 1, 3)
    o = rms_norm(o, lp['dn_norm_w'])
    z_dn = p[..., O_ZDN:O_ZS5].astype(f32).reshape(bsz, length, DN_HEADS, DN_DV)
    y_dn = (o * jax.nn.silu(z_dn)).reshape(bsz, length, DN_WIDTH).astype(h.dtype)

    y = lp['s5_D'].astype(f32).reshape(S5_GROUPS, S5_GROUP) * u
    for d in range(2):
        hr, hi = ssm[d]
        y = y + jnp.einsum('blgp,gcp->blgc', hr, lp['s5_C_re'][d].astype(f32)) \
            - jnp.einsum('blgp,gcp->blgc', hi, lp['s5_C_im'][d].astype(f32))
    y = jax.nn.gelu(y.reshape(bsz, length, S5_WIDTH))
    y = y * jax.nn.sigmoid(y @ lp['glu_w'].astype(f32) + lp['glu_b'].astype(f32))
    z_s5 = p[..., O_ZS5:O_GDN].astype(f32)
    y_s5 = (y * jax.nn.silu(z_s5)).astype(h.dtype)

    merged = jax.nn.sigmoid(p[..., O_GDN:O_GS5]) * (y_dn @ lp['w_proj_dn']) \
        + jax.nn.sigmoid(p[..., O_GS5:IN_WIDTH]) * (y_s5 @ lp['w_proj_s5'])
    return merged @ lp['w_out'], states


def setup_inputs(seed: int = 0) -> dict:
    key = jax.random.key(seed)
    ks = jax.random.split(key, 28)
    f32 = jnp.float32
    nrm = lambda k_, shape, s: jax.random.normal(k_, shape, f32) * s
    dt = jnp.exp(jax.random.uniform(ks[9], (DEPTH, 2, DN_HEADS), f32, math.log(1e-3), math.log(1e-1)))
    lam_im = jnp.pi * jnp.arange(S5_STATE, dtype=f32)
    return {
        'x': nrm(ks[0], (BATCH, SEQ, D_MODEL), 1.0),
        'c': nrm(ks[1], (BATCH, D_MODEL), 1.0),
        'ctx': nrm(ks[2], (BATCH, CTX_LEN, D_MODEL), 1.0),
        'c_ctx': nrm(ks[3], (D_MODEL,), 1.0),
        'w_ada': nrm(ks[4], (DEPTH, D_MODEL, 3 * D_MODEL), 0.5 * D_MODEL ** -0.5),
        'b_ada': nrm(ks[5], (DEPTH, 3 * D_MODEL), 0.01),
        'norm_w': 1.0 + nrm(ks[6], (DEPTH, D_MODEL), 0.02),
        'w_in': nrm(ks[7], (DEPTH, D_MODEL, IN_WIDTH), D_MODEL ** -0.5),
        'conv_w': nrm(ks[8], (DEPTH, CONV_K, QKV_W), CONV_K ** -0.5),
        'dn_A_log': jnp.log(jax.random.uniform(ks[10], (DEPTH, 2, DN_HEADS), f32, 1.0, 16.0)),
        'dn_dt_bias': dt + jnp.log(-jnp.expm1(-dt)),
        'dn_norm_w': 1.0 + nrm(ks[11], (DEPTH, DN_DV), 0.02),
        's5_lam_re': -0.5 + nrm(ks[12], (DEPTH, 2, S5_GROUPS, S5_STATE), 0.01),
        's5_lam_im': lam_im + nrm(ks[13], (DEPTH, 2, S5_GROUPS, S5_STATE), 0.01),
        's5_log_step': jax.random.uniform(ks[14], (DEPTH, 2, S5_GROUPS), f32, math.log(1e-3), math.log(1e-1)),
        's5_B_re': nrm(ks[15], (DEPTH, 2, S5_GROUPS, S5_STATE, S5_GROUP), (2 * S5_GROUP) ** -0.5),
        's5_B_im': nrm(ks[16], (DEPTH, 2, S5_GROUPS, S5_STATE, S5_GROUP), (2 * S5_GROUP) ** -0.5),
        's5_C_re': nrm(ks[17], (DEPTH, 2, S5_GROUPS, S5_GROUP, S5_STATE), (2 * S5_STATE) ** -0.5),
        's5_C_im': nrm(ks[18], (DEPTH, 2, S5_GROUPS, S5_GROUP, S5_STATE), (2 * S5_STATE) ** -0.5),
        's5_D': nrm(ks[19], (DEPTH, S5_WIDTH), 1.0),
        'glu_w': nrm(ks[20], (DEPTH, S5_WIDTH, S5_WIDTH), S5_WIDTH ** -0.5),
        'glu_b': nrm(ks[21], (DEPTH, S5_WIDTH), 0.01),
        'w_proj_dn': nrm(ks[22], (DEPTH, DN_WIDTH, D_MODEL), DN_WIDTH ** -0.5),
        'w_proj_s5': nrm(ks[23], (DEPTH, S5_WIDTH, D_MODEL), S5_WIDTH ** -0.5),
        'w_out': nrm(ks[24], (DEPTH, D_MODEL, D_MODEL), D_MODEL ** -0.5),
        'final_norm_w': 1.0 + nrm(ks[25], (D_MODEL,), 0.02),
    }


def reference(x, c, ctx, c_ctx, w_ada, b_ada, norm_w, w_in, conv_w, dn_A_log, dn_dt_bias, dn_norm_w,
              s5_lam_re, s5_lam_im, s5_log_step, s5_B_re, s5_B_im, s5_C_re, s5_C_im, s5_D, glu_w, glu_b,
              w_proj_dn, w_proj_s5, w_out, final_norm_w):
    n_rows = x.shape[1] // GRID_W
    cx = ctx
    for i in range(DEPTH):
        lp = {'w_in': w_in[i], 'conv_w': conv_w[i], 'dn_A_log': dn_A_log[i], 'dn_dt_bias': dn_dt_bias[i],
              'dn_norm_w': dn_norm_w[i], 's5_lam_re': s5_lam_re[i], 's5_lam_im': s5_lam_im[i],
              's5_log_step': s5_log_step[i], 's5_B_re': s5_B_re[i], 's5_B_im': s5_B_im[i],
              's5_C_re': s5_C_re[i], 's5_C_im': s5_C_im[i], 's5_D': s5_D[i], 'glu_w': glu_w[i],
              'glu_b': glu_b[i], 'w_proj_dn': w_proj_dn[i], 'w_proj_s5': w_proj_s5[i], 'w_out': w_out[i]}
        last = i == DEPTH - 1
        mod_k = jax.nn.silu(c_ctx) @ w_ada[i] + b_ada[i]
        shift_k, scale_k, gate_k = jnp.split(mod_k, 3, axis=-1)
        hc = rms_norm(cx, norm_w[i]) * (1.0 + scale_k) + shift_k
        ctx_out, ctx_states = mixer(hc, lp, None, None, not last)
        if not last:
            cx = cx + gate_k * ctx_out
        mod_x = jax.nn.silu(c) @ w_ada[i] + b_ada[i]
        shift, scale, gate = jnp.split(mod_x[:, None, :], 3, axis=-1)
        h = rms_norm(x, norm_w[i]) * (1.0 + scale) + shift
        out, _ = mixer(h, lp, ctx_states, n_rows, True)
        x = x + gate * out
    return rms_norm(x, final_norm_w)
```

```python
import functools

import jax
import jax.numpy as jnp
from jax import lax
from jax.experimental import pallas as pl
from jax.experimental.pallas import tpu as pltpu

F32 = jnp.float32
BF16 = jnp.bfloat16
HIGHEST = lax.Precision.HIGHEST

D_MODEL = 1024
NORM_EPS = 1e-6
GRID_W = 64
DN_HEADS = 8
DN_DK = 128
DN_DV = 128
DN_WIDTH = DN_HEADS * DN_DV
CONV_K = 5
CHUNK = 64
S5_WIDTH = 512
S5_GROUP = 16
S5_GROUPS = S5_WIDTH // S5_GROUP
S5_STATE = 64
S5_HALF_GROUPS = S5_GROUPS // 2
S5_HALF_STATE = S5_HALF_GROUPS * S5_STATE
S5_LANES = 4 * S5_HALF_STATE
S5_SUB = 64
S5_LOG_SUB = 6
QK_W = DN_HEADS * DN_DK
QKV_W = 2 * QK_W + DN_WIDTH
O_BETA = QKV_W
O_DECAY = O_BETA + 2 * DN_HEADS
O_U = O_DECAY + 2 * DN_HEADS
STATE_COLS = O_U + S5_WIDTH
O_ZDN = STATE_COLS
O_ZS5 = O_ZDN + DN_WIDTH
O_GDN = O_ZS5 + S5_WIDTH
O_GS5 = O_GDN + D_MODEL
IN_WIDTH = O_GS5 + D_MODEL

LANE = 128
BD_PAD = LANE
SMALL_W = BD_PAD + S5_WIDTH
GATE_W = 3 * D_MODEL + S5_WIDTH
COL_TILE = 512
VMEM_LIMIT = 48 * 1024 * 1024


def _sigmoid(x):
    return 1.0 / (1.0 + jnp.exp(-x))


def _softplus(x):
    return jnp.maximum(x, 0.0) + jnp.log1p(jnp.exp(-jnp.abs(x)))


def _dot(a, b):
    return jnp.dot(a, b, preferred_element_type=F32)


def _dot_hi(a, b):
    return jnp.dot(a, b, precision=HIGHEST, preferred_element_type=F32)


def _dot_nt(a, b):
    return lax.dot_general(a, b, (((1,), (1,)), ((), ())), preferred_element_type=F32)


def _dot_tn(a, b):
    return lax.dot_general(a, b, (((0,), (0,)), ((), ())), preferred_element_type=F32)


def _split3(x):
    hi = x.astype(BF16)
    r1 = x - hi.astype(F32)
    mid = r1.astype(BF16)
    lo = (r1 - mid.astype(F32)).astype(BF16)
    return hi, mid, lo


def _dot_sel_rhs(x, sel):
    hi, mid, lo = _split3(x)
    return _dot(hi, sel) + _dot(mid, sel) + _dot(lo, sel)


def _dot_sel_lhs(sel, x):
    hi, mid, lo = _split3(x)
    return _dot(sel, hi) + _dot(sel, mid) + _dot(sel, lo)


def _mod_kernel(c_ref, w_ref, b_ref, o_ref):
    c = c_ref[...]
    s = c * _sigmoid(c)
    o_ref[...] = _dot_hi(s, w_ref[...]) + b_ref[...]


def _mod_vectors(cvec, w_ada, b_ada):
    n = w_ada.shape[1]
    return pl.pallas_call(
        _mod_kernel,
        grid=(n // COL_TILE,),
        in_specs=[
            pl.BlockSpec((8, D_MODEL), lambda j: (0, 0)),
            pl.BlockSpec((D_MODEL, COL_TILE), lambda j: (0, j)),
            pl.BlockSpec((1, COL_TILE), lambda j: (0, j)),
        ],
        out_specs=pl.BlockSpec((8, COL_TILE), lambda j: (0, j)),
        out_shape=jax.ShapeDtypeStruct((8, n), F32),
        compiler_params=pltpu.CompilerParams(dimension_semantics=("arbitrary",)),
        name="mod_vectors",
    )(cvec, w_ada, b_ada)


def _s5_prep_kernel(lre_ref, lim_ref, ls_ref, bre_ref, bim_ref, ar_ref, ai_ref, bbr_ref, bbi_ref):
    lam_re = lre_ref[...]
    lam_im = lim_ref[...]
    dt = jnp.exp(ls_ref[...])
    mag = jnp.exp(lam_re * dt)
    abar_re = mag * jnp.cos(lam_im * dt)
    abar_im = mag * jnp.sin(lam_im * dt)
    num_re = abar_re - 1.0
    num_im = abar_im
    den = lam_re * lam_re + lam_im * lam_im
    f_re = (num_re * lam_re + num_im * lam_im) / den
    f_im = (num_im * lam_re - num_re * lam_im) / den
    b_re = bre_ref[...]
    b_im = bim_ref[...]
    bbr_ref[...] = f_re * b_re - f_im * b_im
    bbi_ref[...] = f_re * b_im + f_im * b_re
    pr, pi = abar_re, abar_im
    for k in range(S5_LOG_SUB):
        ar_ref[k] = pr
        ai_ref[k] = pi
        pr, pi = pr * pr - pi * pi, 2.0 * pr * pi


def _s5_prepare(lam_re, lam_im, log_step, b_re, b_im):
    dg = 2 * S5_GROUPS
    lre = lam_re.reshape(dg, 1, S5_STATE)
    lim = lam_im.reshape(dg, 1, S5_STATE)
    ls = jnp.broadcast_to(log_step.reshape(dg, 1, 1), (dg, 1, S5_STATE))
    bre = jnp.swapaxes(b_re.reshape(dg, S5_STATE, S5_GROUP), 1, 2)
    bim = jnp.swapaxes(b_im.reshape(dg, S5_STATE, S5_GROUP), 1, 2)
    full3 = lambda shape: pl.BlockSpec(shape, lambda: (0,) * len(shape))
    ar, ai, bbr, bbi = pl.pallas_call(
        _s5_prep_kernel,
        in_specs=[full3(lre.shape), full3(lim.shape), full3(ls.shape), full3(bre.shape), full3(bim.shape)],
        out_specs=[full3((S5_LOG_SUB, dg, 1, S5_STATE)), full3((S5_LOG_SUB, dg, 1, S5_STATE)),
                   full3(bre.shape), full3(bim.shape)],
        out_shape=[jax.ShapeDtypeStruct((S5_LOG_SUB, dg, 1, S5_STATE), F32),
                   jax.ShapeDtypeStruct((S5_LOG_SUB, dg, 1, S5_STATE), F32),
                   jax.ShapeDtypeStruct(bre.shape, F32), jax.ShapeDtypeStruct(bim.shape, F32)],
        name="s5_discretise",
    )(lre.astype(F32), lim.astype(F32), ls.astype(F32), bre.astype(F32), bim.astype(F32))
    apow_re = ar.reshape(S5_LOG_SUB, 2, S5_GROUPS, S5_STATE)
    apow_im = ai.reshape(S5_LOG_SUB, 2, S5_GROUPS, S5_STATE)
    bbar_re = bbr.reshape(2, S5_GROUPS, S5_GROUP, S5_STATE)
    bbar_im = bbi.reshape(2, S5_GROUPS, S5_GROUP, S5_STATE)
    return apow_re, apow_im, bbar_re, bbar_im


def _s5_lane_layout(re, im):
    lead = re.shape[:-2]
    re = re.reshape(*lead, 2, S5_HALF_STATE)
    im = im.reshape(*lead, 2, S5_HALF_STATE)
    return jnp.concatenate([re, im], axis=-1).reshape(*lead, S5_LANES)


def _s5_block_matrices(bbar_re, bbar_im, c_re, c_im):
    eye = jnp.eye(S5_HALF_GROUPS, dtype=F32)

    def in_map(bb):
        bb = bb.reshape(2, 2, S5_HALF_GROUPS, S5_GROUP, S5_STATE)
        m = jnp.einsum('dhgcp,gk->dhgckp', bb, eye)
        return m.reshape(2, 2, S5_HALF_GROUPS * S5_GROUP, S5_HALF_STATE)

    def out_map(cc):
        cc = cc.reshape(2, 2, S5_HALF_GROUPS, S5_GROUP, S5_STATE)
        m = jnp.einsum('dhgcp,gk->dhgpkc', cc, eye)
        return m.reshape(2, 2, S5_HALF_STATE, S5_HALF_GROUPS * S5_GROUP)

    b_blk = jnp.concatenate([in_map(bbar_re), in_map(bbar_im)], axis=-1)
    c_blk = jnp.concatenate([out_map(c_re), out_map(-c_im)], axis=-2)
    return b_blk.astype(BF16), c_blk.astype(BF16)


def _conv_silu(acc, cw, rows):
    tm = acc.shape[0]
    pos = lax.broadcasted_iota(jnp.int32, acc.shape, 0) & (rows - 1)
    half = CONV_K // 2
    y = acc * cw[half:half + 1, :]
    for j in range(CONV_K):
        s = j - half
        if s == 0:
            continue
        shifted = pltpu.roll(acc, (-s) % tm, axis=0)
        valid = (pos >= -s) if s < 0 else (pos < rows - s)
        y = y + jnp.where(valid, shifted, 0.0) * cw[j:j + 1, :]
    return y * _sigmoid(y)


def _l2norm_heads(y, scale):
    outs = []
    for h in range(y.shape[1] // DN_DK):
        t = y[:, h * DN_DK:(h + 1) * DN_DK]
        outs.append(t * (lax.rsqrt(jnp.sum(t * t, axis=-1, keepdims=True) + NORM_EPS) * scale))
    return jnp.concatenate(outs, axis=1)


def _inproj_kernel(*refs, epilogue, rows):
    if epilogue == "qkv":
        x_ref, nw_ref, sc_ref, sh_ref, w_ref, cw_ref, o_ref, h_sc = refs
    else:
        x_ref, nw_ref, sc_ref, sh_ref, w_ref, o_ref, h_sc = refs
    j = pl.program_id(1)

    @pl.when(j == 0)
    def _():
        x = x_ref[...]
        y = x * lax.rsqrt(jnp.mean(x * x, axis=-1, keepdims=True) + NORM_EPS) * nw_ref[...]
        h_sc[...] = (y * (1.0 + sc_ref[...]) + sh_ref[...]).astype(BF16)

    acc = _dot(h_sc[...], w_ref[...])
    if epilogue == "raw":
        o_ref[...] = acc
    elif epilogue == "gate":
        mix = jnp.where(jnp.logical_or(j < 2, j == 6), 1.0, 0.0)
        o_ref[...] = (acc * mix + (1.0 - mix)) * _sigmoid(acc)
    else:
        y = _conv_silu(acc, cw_ref[...], rows)
        n_qk_tiles = QK_W // COL_TILE

        @pl.when(j < 2 * n_qk_tiles)
        def _():
            scale = jnp.where(j < n_qk_tiles, DN_DK ** -0.5, 1.0)
            o_ref[...] = _l2norm_heads(y, scale)

        @pl.when(j >= 2 * n_qk_tiles)
        def _():
            o_ref[...] = y


def _inproj(x2, norm_w, mod3, mod_row, w, conv_w, *, epilogue, rows, tm):
    n = x2.shape[0]
    width = w.shape[1]
    tn = COL_TILE if width % COL_TILE == 0 else width
    grid = (n // tm, width // tn)
    vec = lambda k: pl.BlockSpec((None, 1, D_MODEL), lambda i, j: (mod_row(i), 0, k))
    in_specs = [
        pl.BlockSpec((tm, D_MODEL), lambda i, j: (i, 0)),
        pl.BlockSpec((1, D_MODEL), lambda i, j: (0, 0)),
        vec(1),
        vec(0),
        pl.BlockSpec((D_MODEL, tn), lambda i, j: (0, j)),
    ]
    args = [x2, norm_w, mod3, mod3, w]
    if epilogue == "qkv":
        in_specs.append(pl.BlockSpec((8, tn), lambda i, j: (0, j)))
        args.append(conv_w)
    return pl.pallas_call(
        functools.partial(_inproj_kernel, epilogue=epilogue, rows=rows),
        grid=grid,
        in_specs=in_specs,
        out_specs=pl.BlockSpec((tm, tn), lambda i, j: (i, j)),
        out_shape=jax.ShapeDtypeStruct((n, width), F32),
        scratch_shapes=[pltpu.VMEM((tm, D_MODEL), BF16)],
        compiler_params=pltpu.CompilerParams(dimension_semantics=("arbitrary", "arbitrary"),
                                             vmem_limit_bytes=VMEM_LIMIT),
        name="inproj_" + epilogue,
    )(*args)


def _block_masks(n, reverse):
    ri = lax.broadcasted_iota(jnp.int32, (n, n), 0)
    ci = lax.broadcasted_iota(jnp.int32, (n, n), 1)
    lo, hi = (ri, ci) if reverse else (ci, ri)
    masks = []
    b = 1
    while b < n:
        masks.append(((hi ^ lo) < 2 * b) & ((hi & b) != 0) & ((lo & b) == 0))
        b *= 2
    return masks


def _unit_tri_inverse(am, masks):
    n = am.shape[0]
    ri = lax.broadcasted_iota(jnp.int32, (n, n), 0)
    ci = lax.broadcasted_iota(jnp.int32, (n, n), 1)
    inv = jnp.where(ri == ci, 1.0, 0.0).astype(F32) - jnp.where(masks[0], am, 0.0)
    for mask in masks[1:]:
        e = jnp.where(mask, am, 0.0)
        inv = inv - _dot_hi(inv, _dot_hi(e, inv))
    return inv


def _delta_kernel(*refs, direction, need_out):
    if need_out:
        q_ref, k_ref, v_ref, bd_ref, par_ref, eb_ref, eg_ref, s0_ref, o_ref, sf_ref, s_sc = refs
    else:
        q_ref, k_ref, v_ref, bd_ref, par_ref, eb_ref, eg_ref, s0_ref, sf_ref, s_sc = refs
    reverse = direction == 1
    c = pl.program_id(1)

    @pl.when(c == 0)
    def _():
        s_sc[...] = s0_ref[...]

    q = q_ref[...]
    k = k_ref[...]
    v = v_ref[...]
    bd = bd_ref[...]
    par = par_ref[...]
    beta_all = _sigmoid(bd)
    g_all = -jnp.exp(par[0:1, :]) * _softplus(bd + par[1:2, :])

    ri = lax.broadcasted_iota(jnp.int32, (CHUNK, CHUNK), 0)
    ci = lax.broadcasted_iota(jnp.int32, (CHUNK, CHUNK), 1)
    incl = (ri <= ci) if reverse else (ri >= ci)
    strict = (ri < ci) if reverse else (ri > ci)
    tri = jnp.where(incl, 1.0, 0.0).astype(BF16)
    masks = _block_masks(CHUNK, reverse)

    g_cum = _dot_sel_lhs(tri, g_all)
    g_cum_t = jnp.transpose(jnp.concatenate([g_cum, jnp.zeros_like(g_cum)], axis=0))
    beta_x = _dot_sel_rhs(beta_all, eb_ref[...])
    g_x = _dot_sel_rhs(g_cum, eg_ref[...])
    edge = 0 if reverse else CHUNK - 1
    g_last = g_x[edge:edge + 1, :]
    e_g = jnp.exp(g_x)
    e_gl = jnp.exp(g_last - g_x)
    a_end = jnp.exp(g_last)
    kb = k * beta_x
    vb = v * beta_x
    kbe = kb * e_g
    qd = q * e_g
    kend = k * e_gl

    outs = []
    for h in range(DN_HEADS):
        sl = slice(h * DN_DK, (h + 1) * DN_DK)
        gate_lane = 2 * DN_HEADS + DN_HEADS * direction + h
        diff = g_x[:, h * DN_DK:h * DN_DK + CHUNK] - g_cum_t[gate_lane:gate_lane + 1, 0:CHUNK]
        dec = jnp.exp(jnp.where(incl, diff, -jnp.inf))
        k_h = k[:, sl]
        am = jnp.where(strict, _dot_nt(kb[:, sl], k_h) * dec, 0.0)
        tinv = _unit_tri_inverse(am, masks)
        u = _dot_hi(tinv, vb[:, sl])
        w = _dot_hi(tinv, kbe[:, sl])
        s = s_sc[h]
        v_new = u - _dot(w, s)
        if need_out:
            attn = _dot_nt(q[:, sl], k_h) * dec
            outs.append(_dot(qd[:, sl], s) + _dot(attn, v_new))
        s_sc[h] = s * a_end[:, sl] + _dot_tn(kend[:, sl], v_new)

    if need_out:
        o_ref[...] = jnp.concatenate(outs, axis=1)

    @pl.when(c == pl.num_programs(1) - 1)
    def _():
        sf_ref[...] = s_sc[...]


def _delta_rule(qkv, small, par, e_beta, e_gate, s0, *, batch, direction, need_out):
    n = qkv.shape[0]
    n_chunks = n // batch // CHUNK
    if direction == 0:
        row = lambda b, c: b * n_chunks + c
    else:
        row = lambda b, c: b * n_chunks + (n_chunks - 1 - c)
    in_specs = [
        pl.BlockSpec((CHUNK, QK_W), lambda b, c: (row(b, c), 0)),
        pl.BlockSpec((CHUNK, QK_W), lambda b, c: (row(b, c), 1)),
        pl.BlockSpec((CHUNK, DN_WIDTH), lambda b, c: (row(b, c), 2)),
        pl.BlockSpec((CHUNK, BD_PAD), lambda b, c: (row(b, c), 0)),
        pl.BlockSpec((8, LANE), lambda b, c: (0, 0)),
        pl.BlockSpec((LANE, DN_WIDTH), lambda b, c: (0, 0)),
        pl.BlockSpec((LANE, DN_WIDTH), lambda b, c: (0, 0)),
        pl.BlockSpec((None, DN_HEADS, DN_DK, DN_DV), lambda b, c: (b, 0, 0, 0)),
    ]
    state_spec = pl.BlockSpec((None, DN_HEADS, DN_DK, DN_DV), lambda b, c: (b, 0, 0, 0))
    state_shape = jax.ShapeDtypeStruct((batch, DN_HEADS, DN_DK, DN_DV), F32)
    if need_out:
        out_specs = [pl.BlockSpec((CHUNK, DN_WIDTH), lambda b, c: (row(b, c), 0)), state_spec]
        out_shape = [jax.ShapeDtypeStruct((n, DN_WIDTH), F32), state_shape]
    else:
        out_specs = [state_spec]
        out_shape = [state_shape]
    res = pl.pallas_call(
        functools.partial(_delta_kernel, direction=direction, need_out=need_out),
        grid=(batch, n_chunks),
        in_specs=in_specs,
        out_specs=out_specs,
        out_shape=out_shape,
        scratch_shapes=[pltpu.VMEM((DN_HEADS, DN_DK, DN_DV), F32)],
        compiler_params=pltpu.CompilerParams(dimension_semantics=("arbitrary", "arbitrary"),
                                             vmem_limit_bytes=VMEM_LIMIT),
        name=f"delta_d{direction}" + ("_out" if need_out else "_state"),
    )(qkv, qkv, qkv, small, par, e_beta, e_gate, s0)
    return res if need_out else (None, res[0])


def _s5_kernel(*refs, reverse, need_out, n_sub):
    if need_out:
        u_ref, b_ref, c_ref, ar_ref, ai_ref, h0_ref, y_ref, hf_ref, x_sc, h_sc, carry_sc = refs
    else:
        u_ref, b_ref, ar_ref, ai_ref, h0_ref, hf_ref, x_sc, carry_sc = refs
    blk = pl.program_id(1)

    @pl.when(blk == 0)
    def _():
        carry_sc[...] = h0_ref[...]

    u = u_ref[:, BD_PAD:BD_PAD + S5_WIDTH].astype(BF16)
    half_u = S5_WIDTH // 2
    for hf in range(2):
        x_sc[:, hf * 2 * S5_HALF_STATE:(hf + 1) * 2 * S5_HALF_STATE] = _dot(
            u[:, hf * half_u:(hf + 1) * half_u], b_ref[hf])

    row = lax.broadcasted_iota(jnp.int32, (S5_SUB, LANE), 0)
    edge = S5_SUB - 1 if reverse else 0
    last = 0 if reverse else S5_SUB - 1
    n_chunk = S5_HALF_STATE // LANE

    def lane_chunk(i, _):
        hf = i // n_chunk
        off_re = pl.multiple_of(hf * (2 * S5_HALF_STATE) + (i % n_chunk) * LANE, LANE)
        off_im = pl.multiple_of(off_re + S5_HALF_STATE, LANE)
        a_re = [ar_ref[k:k + 1, pl.ds(off_re, LANE)] for k in range(S5_LOG_SUB)]
        a_im = [ai_ref[k:k + 1, pl.ds(off_re, LANE)] for k in range(S5_LOG_SUB)]
        c_re = carry_sc[0:1, pl.ds(off_re, LANE)]
        c_im = carry_sc[0:1, pl.ds(off_im, LANE)]
        order = range(n_sub - 1, -1, -1) if reverse else range(n_sub)
        for sb in order:
            rows = pl.ds(sb * S5_SUB, S5_SUB)
            h_re = x_sc[rows, pl.ds(off_re, LANE)]
            h_im = x_sc[rows, pl.ds(off_im, LANE)]
            in_re = a_re[0] * c_re - a_im[0] * c_im
            in_im = a_re[0] * c_im + a_im[0] * c_re
            h_re = h_re + jnp.where(row == edge, in_re, 0.0)
            h_im = h_im + jnp.where(row == edge, in_im, 0.0)
            for k in range(S5_LOG_SUB):
                d = 1 << k
                if reverse:
                    valid = row < S5_SUB - d
                    p_re = pltpu.roll(h_re, S5_SUB - d, axis=0)
                    p_im = pltpu.roll(h_im, S5_SUB - d, axis=0)
                else:
                    valid = row >= d
                    p_re = pltpu.roll(h_re, d, axis=0)
                    p_im = pltpu.roll(h_im, d, axis=0)
                p_re = jnp.where(valid, p_re, 0.0)
                p_im = jnp.where(valid, p_im, 0.0)
                h_re, h_im = (h_re + a_re[k] * p_re - a_im[k] * p_im,
                              h_im + a_re[k] * p_im + a_im[k] * p_re)
            c_re = h_re[last:last + 1, :]
            c_im = h_im[last:last + 1, :]
            if need_out:
                h_sc[rows, pl.ds(off_re, LANE)] = h_re.astype(BF16)
                h_sc[rows, pl.ds(off_im, LANE)] = h_im.astype(BF16)
        carry_sc[0:1, pl.ds(off_re, LANE)] = c_re
        carry_sc[0:1, pl.ds(off_im, LANE)] = c_im
        return 0

    lax.fori_loop(0, 2 * n_chunk, lane_chunk, 0)

    if need_out:
        for hf in range(2):
            y_ref[:, hf * half_u:(hf + 1) * half_u] = _dot(
                h_sc[:, hf * 2 * S5_HALF_STATE:(hf + 1) * 2 * S5_HALF_STATE], c_ref[hf])

    @pl.when(blk == pl.num_programs(1) - 1)
    def _():
        hf_ref[...] = carry_sc[...]


def _s5_scan(small, b_blk, c_blk, apow_re, apow_im, h0, *, batch, direction, need_out):
    n = small.shape[0]
    length = n // batch
    tb = 256 if length % 256 == 0 else length
    n_blocks = length // tb
    reverse = direction == 1
    if reverse:
        row = lambda b, i: b * n_blocks + (n_blocks - 1 - i)
    else:
        row = lambda b, i: b * n_blocks + i
    const2 = lambda shape: pl.BlockSpec(shape, lambda b, i: (0,) * len(shape))
    in_specs = [pl.BlockSpec((tb, SMALL_W), lambda b, i: (row(b, i), 0)),
                const2(b_blk.shape)]
    args = [small, b_blk]
    if need_out:
        in_specs.append(const2(c_blk.shape))
        args.append(c_blk)
    in_specs += [const2(apow_re.shape), const2(apow_im.shape),
                 pl.BlockSpec((None, 8, S5_LANES), lambda b, i: (b, 0, 0))]
    args += [apow_re, apow_im, h0]
    carry_spec = pl.BlockSpec((None, 8, S5_LANES), lambda b, i: (b, 0, 0))
    carry_shape = jax.ShapeDtypeStruct((batch, 8, S5_LANES), F32)
    scratch = [pltpu.VMEM((tb, S5_LANES), F32)]
    if need_out:
        out_specs = [pl.BlockSpec((tb, S5_WIDTH), lambda b, i: (row(b, i), 0)), carry_spec]
        out_shape = [jax.ShapeDtypeStruct((n, S5_WIDTH), F32), carry_shape]
        scratch.append(pltpu.VMEM((tb, S5_LANES), BF16))
    else:
        out_specs = [carry_spec]
        out_shape = [carry_shape]
    scratch.append(pltpu.VMEM((8, S5_LANES), F32))
    res = pl.pallas_call(
        functools.partial(_s5_kernel, reverse=reverse, need_out=need_out, n_sub=tb // S5_SUB),
        grid=(batch, n_blocks),
        in_specs=in_specs,
        out_specs=out_specs,
        out_shape=out_shape,
        scratch_shapes=scratch,
        compiler_params=pltpu.CompilerParams(dimension_semantics=("arbitrary", "arbitrary"),
                                             vmem_limit_bytes=VMEM_LIMIT),
        name=f"s5_d{direction}" + ("_out" if need_out else "_state"),
    )(*args)
    return res if need_out else (None, res[0])


def _out_kernel(x_ref, gate_ref, of_ref, ob_ref, dnw_ref, szdn_ref, sgdn_ref, sgs5_ref, szs5_ref,
                yf_ref, yb_ref, small_ref, d_ref, gluw_ref, glub_ref, wpd_ref, wps_ref, wo_ref, fnw_ref,
                o_ref):
    o = of_ref[...] + ob_ref[...]
    parts = []
    for h in range(DN_HEADS):
        t = o[:, h * DN_DV:(h + 1) * DN_DV]
        parts.append(t * lax.rsqrt(jnp.mean(t * t, axis=-1, keepdims=True) + NORM_EPS))
    y_dn = jnp.concatenate(parts, axis=1) * dnw_ref[...] * szdn_ref[...]
    u = small_ref[:, BD_PAD:BD_PAD + S5_WIDTH]
    y = d_ref[...] * u + yf_ref[...] + yb_ref[...]
    y = 0.5 * y * (1.0 + jnp.tanh(0.7978845608028654 * (y + 0.044715 * (y * y * y))))
    y = y * _sigmoid(_dot(y.astype(BF16), gluw_ref[...]) + glub_ref[...])
    y_s5 = y * szs5_ref[...]
    merged = (sgdn_ref[...] * _dot(y_dn.astype(BF16), wpd_ref[...])
              + sgs5_ref[...] * _dot(y_s5.astype(BF16), wps_ref[...]))
    out = _dot(merged.astype(BF16), wo_ref[...])
    xo = x_ref[...] + gate_ref[...] * out
    o_ref[...] = xo * lax.rsqrt(jnp.mean(xo * xo, axis=-1, keepdims=True) + NORM_EPS) * fnw_ref[...]


def _output_stage(x2, mod3, o_f, o_b, dn_norm_w, gates, y_f, y_b, small, s5_d, glu_w, glu_b,
                  w_proj_dn, w_proj_s5, w_out, final_norm_w, *, batch, tm):
    n = x2.shape[0]
    tiles_per_batch = n // batch // tm
    tok = lambda width, col: pl.BlockSpec((tm, width), lambda i: (i, col))
    const = lambda shape: pl.BlockSpec(shape, lambda i: (0, 0))
    in_specs = [
        tok(D_MODEL, 0),
        pl.BlockSpec((None, 1, D_MODEL), lambda i: (i // tiles_per_batch, 0, 2)),
        tok(DN_WIDTH, 0), tok(DN_WIDTH, 0), const((1, DN_WIDTH)),
        tok(D_MODEL, 0), tok(D_MODEL, 1), tok(D_MODEL, 2), tok(S5_WIDTH, 3 * D_MODEL // S5_WIDTH),
        tok(S5_WIDTH, 0), tok(S5_WIDTH, 0), tok(SMALL_W, 0), const((1, S5_WIDTH)),
        const((S5_WIDTH, S5_WIDTH)), const((1, S5_WIDTH)),
        const((DN_WIDTH, D_MODEL)), const((S5_WIDTH, D_MODEL)), const((D_MODEL, D_MODEL)),
        const((1, D_MODEL)),
    ]
    return pl.pallas_call(
        _out_kernel,
        grid=(n // tm,),
        in_specs=in_specs,
        out_specs=pl.BlockSpec((tm, D_MODEL), lambda i: (i, 0)),
        out_shape=jax.ShapeDtypeStruct((n, D_MODEL), F32),
        compiler_params=pltpu.CompilerParams(dimension_semantics=("arbitrary",),
                                             vmem_limit_bytes=VMEM_LIMIT),
        name="output_stage",
    )(x2, mod3, o_f, o_b, dn_norm_w, gates, gates, gates, gates, y_f, y_b, small, s5_d,
      glu_w, glu_b, w_proj_dn, w_proj_s5, w_out, final_norm_w)


def _gate_lane_params(a_log, dt_bias):
    par = jnp.zeros((8, LANE), F32)
    par = par.at[0, 2 * DN_HEADS:4 * DN_HEADS].set(a_log.reshape(-1).astype(F32))
    par = par.at[1, 2 * DN_HEADS:4 * DN_HEADS].set(dt_bias.reshape(-1).astype(F32))
    return par


def _expand_matrix(first_lane):
    lane = jnp.arange(LANE)[:, None]
    head = jnp.arange(DN_WIDTH)[None, :] // DN_DV
    return (lane == first_lane + head).astype(BF16)


def kernel(x, c, ctx, c_ctx, w_ada, b_ada, norm_w, w_in, conv_w, dn_A_log, dn_dt_bias, dn_norm_w,
           s5_lam_re, s5_lam_im, s5_log_step, s5_B_re, s5_B_im, s5_C_re, s5_C_im, s5_D, glu_w, glu_b,
           w_proj_dn, w_proj_s5, w_out, final_norm_w):
    batch, seq, d_model = x.shape
    ctx_len = ctx.shape[1]
    depth = w_in.shape[0]
    assert d_model == D_MODEL and depth == 1 and batch <= 4
    assert seq % 512 == 0 and ctx_len % CHUNK == 0 and ctx_len & (ctx_len - 1) == 0
    i = 0
    x2 = x.reshape(batch * seq, D_MODEL)
    ctx2 = ctx.reshape(batch * ctx_len, D_MODEL)

    w_full = w_in[i]
    w_qkv = w_full[:, :QKV_W].astype(BF16)
    w_small = jnp.concatenate(
        [w_full[:, O_BETA:O_U], jnp.zeros((D_MODEL, BD_PAD - (O_U - O_BETA)), w_full.dtype),
         w_full[:, O_U:STATE_COLS]], axis=1).astype(BF16)
    w_gate = jnp.concatenate(
        [w_full[:, O_ZDN:O_ZS5], w_full[:, O_GDN:O_GS5], w_full[:, O_GS5:IN_WIDTH],
         w_full[:, O_ZS5:O_GDN]], axis=1).astype(BF16)
    conv_w8 = jnp.concatenate([conv_w[i], jnp.zeros((8 - CONV_K, QKV_W), conv_w.dtype)], axis=0).astype(F32)
    nw = norm_w[i].reshape(1, D_MODEL).astype(F32)
    e_beta = [_expand_matrix(DN_HEADS * d) for d in range(2)]
    e_gate = [_expand_matrix(2 * DN_HEADS + DN_HEADS * d) for d in range(2)]
    par = _gate_lane_params(dn_A_log[i], dn_dt_bias[i])

    cvec = jnp.zeros((8, D_MODEL), F32).at[:batch].set(c.astype(F32)).at[batch].set(c_ctx.astype(F32))
    mod = _mod_vectors(cvec, w_ada[i].astype(F32), b_ada[i].reshape(1, -1).astype(F32))
    mod3 = mod.reshape(8, 1, 3 * D_MODEL)

    apow_re, apow_im, bbar_re, bbar_im = _s5_prepare(
        s5_lam_re[i], s5_lam_im[i], s5_log_step[i], s5_B_re[i], s5_B_im[i])
    b_blk, c_blk = _s5_block_matrices(
        bbar_re, bbar_im,
        s5_C_re[i].astype(F32), s5_C_im[i].astype(F32))
    apow_re_l = _s5_lane_layout(apow_re, apow_re)
    apow_im_l = _s5_lane_layout(apow_im, apow_im)
    pad8 = lambda t: jnp.concatenate([t, jnp.zeros((8 - t.shape[0],) + t.shape[1:], t.dtype)], axis=0)
    ar_tab = [pad8(apow_re_l[:, d]) for d in range(2)]
    ai_tab = [pad8(apow_im_l[:, d]) for d in range(2)]

    ctx_row = lambda t: batch
    ctx_qkv = _inproj(ctx2, nw, mod3, ctx_row, w_qkv, conv_w8, epilogue="qkv", rows=ctx_len, tm=ctx_len)
    ctx_small = _inproj(ctx2, nw, mod3, ctx_row, w_small, None, epilogue="raw", rows=ctx_len, tm=ctx_len)
    zero_state = jnp.zeros((batch, DN_HEADS, DN_DK, DN_DV), F32)
    zero_h = jnp.zeros((batch, 8, S5_LANES), F32)
    s_ctx, h_ctx = [], []
    for d in range(2):
        _, s_fin = _delta_rule(ctx_qkv, ctx_small, par, e_beta[d], e_gate[d], zero_state,
                               batch=batch, direction=d, need_out=False)
        _, h_fin = _s5_scan(ctx_small, b_blk[d], None, ar_tab[d], ai_tab[d], zero_h,
                            batch=batch, direction=d, need_out=False)
        s_ctx.append(s_fin)
        h_ctx.append(h_fin)

    tm = 512
    tiles_per_batch = seq // tm
    lat_row = lambda t: t // tiles_per_batch
    qkv = _inproj(x2, nw, mod3, lat_row, w_qkv, conv_w8, epilogue="qkv", rows=GRID_W, tm=tm)
    small = _inproj(x2, nw, mod3, lat_row, w_small, None, epilogue="raw", rows=GRID_W, tm=tm)
    gates = _inproj(x2, nw, mod3, lat_row, w_gate, None, epilogue="gate", rows=GRID_W, tm=tm)
    o_dir, y_dir = [], []
    for d in range(2):
        o_d, _ = _delta_rule(qkv, small, par, e_beta[d], e_gate[d], s_ctx[d],
                             batch=batch, direction=d, need_out=True)
        y_d, _ = _s5_scan(small, b_blk[d], c_blk[d], ar_tab[d], ai_tab[d], h_ctx[d],
                          batch=batch, direction=d, need_out=True)
        o_dir.append(o_d)
        y_dir.append(y_d)

    out = _output_stage(
        x2, mod3, o_dir[0], o_dir[1], jnp.tile(dn_norm_w[i].astype(F32), DN_HEADS).reshape(1, DN_WIDTH),
        gates, y_dir[0], y_dir[1], small, s5_D[i].reshape(1, S5_WIDTH).astype(F32),
        glu_w[i].astype(BF16), glu_b[i].reshape(1, S5_WIDTH).astype(F32),
        w_proj_dn[i].astype(BF16), w_proj_s5[i].astype(BF16), w_out[i].astype(BF16),
        final_norm_w.reshape(1, D_MODEL).astype(F32), batch=batch, tm=256)
    return out.reshape(batch, seq, D_MODEL)
```

```python
import functools

import jax
import jax.numpy as jnp
from jax import lax
from jax.experimental import pallas as pl
from jax.experimental.pallas import tpu as pltpu

F32 = jnp.float32
BF16 = jnp.bfloat16
HIGHEST = lax.Precision.HIGHEST

D_MODEL = 1024
NORM_EPS = 1e-6
GRID_W = 64
DN_HEADS = 8
DN_DK = 128
DN_DV = 128
DN_WIDTH = DN_HEADS * DN_DV
CONV_K = 5
CHUNK = 64
S5_WIDTH = 512
S5_GROUP = 16
S5_GROUPS = S5_WIDTH // S5_GROUP
S5_STATE = 64
S5_HALF_GROUPS = S5_GROUPS // 2
S5_HALF_STATE = S5_HALF_GROUPS * S5_STATE
S5_LANES = 4 * S5_HALF_STATE
S5_BLOCK = 256
S5_SEG = 8
S5_STEPS = S5_BLOCK // S5_SEG
S5_SEG_LOG = 3
S5_UNROLL = 4
QK_W = DN_HEADS * DN_DK
QKV_W = 2 * QK_W + DN_WIDTH
O_BETA = QKV_W
O_DECAY = O_BETA + 2 * DN_HEADS
O_U = O_DECAY + 2 * DN_HEADS
STATE_COLS = O_U + S5_WIDTH
O_ZDN = STATE_COLS
O_ZS5 = O_ZDN + DN_WIDTH
O_GDN = O_ZS5 + S5_WIDTH
O_GS5 = O_GDN + D_MODEL
IN_WIDTH = O_GS5 + D_MODEL

LANE = 128
BD_PAD = LANE
SMALL_W = BD_PAD + S5_WIDTH
GATE_W = 3 * D_MODEL + S5_WIDTH
COL_TILE = 512
VMEM_LIMIT = 48 * 1024 * 1024


def _sigmoid(x):
    return 1.0 / (1.0 + jnp.exp(-x))


def _softplus(x):
    return jnp.maximum(x, 0.0) + jnp.log1p(jnp.exp(-jnp.abs(x)))


def _dot(a, b):
    return jnp.dot(a, b, preferred_element_type=F32)


def _dot_hi(a, b):
    return jnp.dot(a, b, precision=HIGHEST, preferred_element_type=F32)


def _dot_nt(a, b):
    return lax.dot_general(a, b, (((1,), (1,)), ((), ())), preferred_element_type=F32)


def _dot_tn(a, b):
    return lax.dot_general(a, b, (((0,), (0,)), ((), ())), preferred_element_type=F32)


def _split3(x):
    hi = x.astype(BF16)
    r1 = x - hi.astype(F32)
    mid = r1.astype(BF16)
    lo = (r1 - mid.astype(F32)).astype(BF16)
    return hi, mid, lo


def _dot_sel_rhs(x, sel):
    hi, mid, lo = _split3(x)
    return _dot(hi, sel) + _dot(mid, sel) + _dot(lo, sel)


def _dot_sel_lhs(sel, x):
    hi, mid, lo = _split3(x)
    return _dot(sel, hi) + _dot(sel, mid) + _dot(sel, lo)


def _mod_kernel(c_ref, w_ref, b_ref, o_ref):
    c = c_ref[...]
    s = c * _sigmoid(c)
    o_ref[...] = _dot_hi(s, w_ref[...]) + b_ref[...]


def _mod_vectors(cvec, w_ada, b_ada):
    n = w_ada.shape[1]
    return pl.pallas_call(
        _mod_kernel,
        grid=(n // COL_TILE,),
        in_specs=[
            pl.BlockSpec((8, D_MODEL), lambda j: (0, 0)),
            pl.BlockSpec((D_MODEL, COL_TILE), lambda j: (0, j)),
            pl.BlockSpec((1, COL_TILE), lambda j: (0, j)),
        ],
        out_specs=pl.BlockSpec((8, COL_TILE), lambda j: (0, j)),
        out_shape=jax.ShapeDtypeStruct((8, n), F32),
        compiler_params=pltpu.CompilerParams(dimension_semantics=("arbitrary",)),
        name="mod_vectors",
    )(cvec, w_ada, b_ada)


def _s5_prep_kernel(lre_ref, lim_ref, ls_ref, bre_ref, bim_ref, pr_ref, pi_ref, bbr_ref, bbi_ref):
    lam_re = lre_ref[...]
    lam_im = lim_ref[...]
    dt = jnp.exp(ls_ref[...])
    mag = jnp.exp(lam_re * dt)
    abar_re = mag * jnp.cos(lam_im * dt)
    abar_im = mag * jnp.sin(lam_im * dt)
    num_re = abar_re - 1.0
    num_im = abar_im
    den = lam_re * lam_re + lam_im * lam_im
    f_re = (num_re * lam_re + num_im * lam_im) / den
    f_im = (num_im * lam_re - num_re * lam_im) / den
    b_re = bre_ref[...]
    b_im = bim_ref[...]
    bbr_ref[...] = f_re * b_re - f_im * b_im
    bbi_ref[...] = f_re * b_im + f_im * b_re
    pr_ref[0] = abar_re
    pi_ref[0] = abar_im
    pr, pi = abar_re, abar_im
    n = 1
    while n < S5_STEPS:
        pr, pi = pr * pr - pi * pi, 2.0 * pr * pi
        n *= 2
    for k in range(S5_SEG_LOG):
        pr_ref[1 + k] = pr
        pi_ref[1 + k] = pi
        pr, pi = pr * pr - pi * pi, 2.0 * pr * pi


def _s5_prepare(lam_re, lam_im, log_step, b_re, b_im):
    dg = 2 * S5_GROUPS
    lre = lam_re.reshape(dg, 1, S5_STATE)
    lim = lam_im.reshape(dg, 1, S5_STATE)
    ls = jnp.broadcast_to(log_step.reshape(dg, 1, 1), (dg, 1, S5_STATE))
    bre = jnp.swapaxes(b_re.reshape(dg, S5_STATE, S5_GROUP), 1, 2)
    bim = jnp.swapaxes(b_im.reshape(dg, S5_STATE, S5_GROUP), 1, 2)
    full3 = lambda shape: pl.BlockSpec(shape, lambda: (0,) * len(shape))
    pow_shape = (1 + S5_SEG_LOG, dg, 1, S5_STATE)
    pr, pi, bbr, bbi = pl.pallas_call(
        _s5_prep_kernel,
        in_specs=[full3(lre.shape), full3(lim.shape), full3(ls.shape), full3(bre.shape), full3(bim.shape)],
        out_specs=[full3(pow_shape), full3(pow_shape), full3(bre.shape), full3(bim.shape)],
        out_shape=[jax.ShapeDtypeStruct(pow_shape, F32), jax.ShapeDtypeStruct(pow_shape, F32),
                   jax.ShapeDtypeStruct(bre.shape, F32), jax.ShapeDtypeStruct(bim.shape, F32)],
        name="s5_discretise",
    )(lre.astype(F32), lim.astype(F32), ls.astype(F32), bre.astype(F32), bim.astype(F32))
    per_dir = lambda t: t.reshape(t.shape[0], 2, S5_GROUPS, S5_STATE)
    bbar_re = bbr.reshape(2, S5_GROUPS, S5_GROUP, S5_STATE)
    bbar_im = bbi.reshape(2, S5_GROUPS, S5_GROUP, S5_STATE)
    return per_dir(pr), per_dir(pi), bbar_re, bbar_im


def _s5_lane_layout(re, im):
    lead = re.shape[:-2]
    re = re.reshape(*lead, 2, S5_HALF_STATE)
    im = im.reshape(*lead, 2, S5_HALF_STATE)
    return jnp.concatenate([re, im], axis=-1).reshape(*lead, S5_LANES)


def _s5_block_matrices(bbar_re, bbar_im, c_re, c_im):
    eye = jnp.eye(S5_HALF_GROUPS, dtype=F32)

    def in_map(bb):
        bb = bb.reshape(2, 2, S5_HALF_GROUPS, S5_GROUP, S5_STATE)
        m = jnp.einsum('dhgcp,gk->dhgckp', bb, eye)
        return m.reshape(2, 2, S5_HALF_GROUPS * S5_GROUP, S5_HALF_STATE)

    def out_map(cc):
        cc = cc.reshape(2, 2, S5_HALF_GROUPS, S5_GROUP, S5_STATE)
        m = jnp.einsum('dhgcp,gk->dhgpkc', cc, eye)
        return m.reshape(2, 2, S5_HALF_STATE, S5_HALF_GROUPS * S5_GROUP)

    b_blk = jnp.concatenate([in_map(bbar_re), in_map(bbar_im)], axis=-1)
    c_blk = jnp.concatenate([out_map(c_re), out_map(-c_im)], axis=-2)
    return b_blk.astype(BF16), c_blk.astype(BF16)


def _conv_silu(acc, cw, rows):
    tm = acc.shape[0]
    pos = lax.broadcasted_iota(jnp.int32, acc.shape, 0) & (rows - 1)
    half = CONV_K // 2
    y = acc * cw[half:half + 1, :]
    for j in range(CONV_K):
        s = j - half
        if s == 0:
            continue
        shifted = pltpu.roll(acc, (-s) % tm, axis=0)
        valid = (pos >= -s) if s < 0 else (pos < rows - s)
        y = y + jnp.where(valid, shifted, 0.0) * cw[j:j + 1, :]
    return y * _sigmoid(y)


def _l2norm_heads(y, scale):
    outs = []
    for h in range(y.shape[1] // DN_DK):
        t = y[:, h * DN_DK:(h + 1) * DN_DK]
        outs.append(t * (lax.rsqrt(jnp.sum(t * t, axis=-1, keepdims=True) + NORM_EPS) * scale))
    return jnp.concatenate(outs, axis=1)


def _inproj_kernel(*refs, epilogue, rows):
    if epilogue == "qkv":
        x_ref, nw_ref, sc_ref, sh_ref, w_ref, cw_ref, o_ref, h_sc = refs
    else:
        x_ref, nw_ref, sc_ref, sh_ref, w_ref, o_ref, h_sc = refs
    j = pl.program_id(1)

    @pl.when(j == 0)
    def _():
        x = x_ref[...]
        y = x * lax.rsqrt(jnp.mean(x * x, axis=-1, keepdims=True) + NORM_EPS) * nw_ref[...]
        h_sc[...] = (y * (1.0 + sc_ref[...]) + sh_ref[...]).astype(BF16)

    acc = _dot(h_sc[...], w_ref[...])
    if epilogue == "raw":
        o_ref[...] = acc
    elif epilogue == "gate":
        mix = jnp.where(jnp.logical_or(j < 2, j == 6), 1.0, 0.0)
        o_ref[...] = (acc * mix + (1.0 - mix)) * _sigmoid(acc)
    else:
        y = _conv_silu(acc, cw_ref[...], rows)
        n_qk_tiles = QK_W // COL_TILE

        @pl.when(j < 2 * n_qk_tiles)
        def _():
            scale = jnp.where(j < n_qk_tiles, DN_DK ** -0.5, 1.0)
            o_ref[...] = _l2norm_heads(y, scale)

        @pl.when(j >= 2 * n_qk_tiles)
        def _():
            o_ref[...] = y


def _inproj(x2, norm_w, mod3, mod_row, w, conv_w, *, epilogue, rows, tm):
    n = x2.shape[0]
    width = w.shape[1]
    tn = COL_TILE if width % COL_TILE == 0 else width
    grid = (n // tm, width // tn)
    vec = lambda k: pl.BlockSpec((None, 1, D_MODEL), lambda i, j: (mod_row(i), 0, k))
    in_specs = [
        pl.BlockSpec((tm, D_MODEL), lambda i, j: (i, 0)),
        pl.BlockSpec((1, D_MODEL), lambda i, j: (0, 0)),
        vec(1),
        vec(0),
        pl.BlockSpec((D_MODEL, tn), lambda i, j: (0, j)),
    ]
    args = [x2, norm_w, mod3, mod3, w]
    if epilogue == "qkv":
        in_specs.append(pl.BlockSpec((8, tn), lambda i, j: (0, j)))
        args.append(conv_w)
    return pl.pallas_call(
        functools.partial(_inproj_kernel, epilogue=epilogue, rows=rows),
        grid=grid,
        in_specs=in_specs,
        out_specs=pl.BlockSpec((tm, tn), lambda i, j: (i, j)),
        out_shape=jax.ShapeDtypeStruct((n, width), F32),
        scratch_shapes=[pltpu.VMEM((tm, D_MODEL), BF16)],
        compiler_params=pltpu.CompilerParams(dimension_semantics=("arbitrary", "arbitrary"),
                                             vmem_limit_bytes=VMEM_LIMIT),
        name="inproj_" + epilogue,
    )(*args)


def _block_masks(n, reverse):
    ri = lax.broadcasted_iota(jnp.int32, (n, n), 0)
    ci = lax.broadcasted_iota(jnp.int32, (n, n), 1)
    lo, hi = (ri, ci) if reverse else (ci, ri)
    masks = []
    b = 1
    while b < n:
        masks.append(((hi ^ lo) < 2 * b) & ((hi & b) != 0) & ((lo & b) == 0))
        b *= 2
    return masks


def _delta_chunk_prep(q, k, v, bd, par, e_beta, e_gate, direction):
    reverse = direction == 1
    beta_all = _sigmoid(bd)
    g_all = -jnp.exp(par[0:1, :]) * _softplus(bd + par[1:2, :])
    ri = lax.broadcasted_iota(jnp.int32, (CHUNK, CHUNK), 0)
    ci = lax.broadcasted_iota(jnp.int32, (CHUNK, CHUNK), 1)
    incl = (ri <= ci) if reverse else (ri >= ci)
    strict = (ri < ci) if reverse else (ri > ci)
    tri = jnp.where(incl, 1.0, 0.0).astype(BF16)

    g_cum = _dot_sel_lhs(tri, g_all)
    g_cum_t = jnp.transpose(jnp.concatenate([g_cum, jnp.zeros_like(g_cum)], axis=0))
    beta_x = _dot(beta_all.astype(BF16), e_beta)
    g_hi = g_cum.astype(BF16)
    g_x = _dot(g_hi, e_gate) + _dot((g_cum - g_hi.astype(F32)).astype(BF16), e_gate)
    edge = 0 if reverse else CHUNK - 1
    g_last = g_x[edge:edge + 1, :]
    e_g = jnp.exp(g_x)
    a_end = jnp.exp(g_last)
    kb = k * beta_x
    k_bf = k.astype(BF16)
    kq = jnp.concatenate([kb, q], axis=0).astype(BF16)
    vb = (v * beta_x).astype(BF16)
    kbe = (kb * e_g).astype(BF16)
    qd = (q * e_g).astype(BF16)
    kend = (k * jnp.exp(g_last - g_x)).astype(BF16)

    heads = []
    for h in range(DN_HEADS):
        sl = slice(h * DN_DK, (h + 1) * DN_DK)
        gate_lane = 2 * DN_HEADS + DN_HEADS * direction + h
        diff = g_x[:, h * DN_DK:h * DN_DK + CHUNK] - g_cum_t[gate_lane:gate_lane + 1, 0:CHUNK]
        heads.append(dict(
            dec=jnp.exp(jnp.where(incl, diff, -jnp.inf)),
            kq=kq[:, sl], k=k_bf[:, sl],
            rhs=jnp.concatenate([vb[:, sl], kbe[:, sl]], axis=1),
            qd=qd[:, sl], kend=kend[:, sl], a_end=a_end[:, sl]))
    return heads, strict, _block_masks(CHUNK, reverse)


def _delta_kernel(*refs, need_out):
    ins, rest = refs[:14], refs[14:]
    (qf_ref, kf_ref, vf_ref, bdf_ref, qb_ref, kb_ref, vb_ref, bdb_ref,
     par_ref, eb0_ref, eg0_ref, eb1_ref, eg1_ref, s0_ref) = ins
    if need_out:
        of_ref, ob_ref, sf_ref, s_sc = rest
    else:
        sf_ref, s_sc = rest
    c = pl.program_id(1)

    @pl.when(c == 0)
    def _():
        s_sc[...] = s0_ref[...]

    par = par_ref[...]
    units = []
    for d, (q_ref, k_ref, v_ref, bd_ref, eb_ref, eg_ref) in enumerate(
            ((qf_ref, kf_ref, vf_ref, bdf_ref, eb0_ref, eg0_ref),
             (qb_ref, kb_ref, vb_ref, bdb_ref, eb1_ref, eg1_ref))):
        heads, strict, masks = _delta_chunk_prep(q_ref[...], k_ref[...], v_ref[...], bd_ref[...], par,
                                                 eb_ref[...], eg_ref[...], d)
        for h, unit in enumerate(heads):
            unit.update(d=d, h=h, strict=strict, masks=masks)
            units.append(unit)

    ri = lax.broadcasted_iota(jnp.int32, (CHUNK, CHUNK), 0)
    ci = lax.broadcasted_iota(jnp.int32, (CHUNK, CHUNK), 1)
    eye = jnp.where(ri == ci, 1.0, 0.0).astype(F32)
    for u in units:
        aq = _dot_nt(u["kq"], u["k"])
        u["am"] = jnp.where(u["strict"], aq[:CHUNK] * u["dec"], 0.0)
        u["attn"] = (aq[CHUNK:] * u["dec"]).astype(BF16)
        u["inv"] = eye - jnp.where(u["masks"][0], u["am"], 0.0)
    for level in range(1, len(units[0]["masks"])):
        for u in units:
            e = jnp.where(u["masks"][level], u["am"], 0.0).astype(BF16)
            u["inv_bf"] = u["inv"].astype(BF16)
            u["t"] = _dot(e, u["inv_bf"]).astype(BF16)
        for u in units:
            u["inv"] = u["inv"] - _dot(u["inv_bf"], u["t"])
    for u in units:
        u["uw"] = _dot(u["inv"].astype(BF16), u["rhs"])
    for u in units:
        u["s"] = s_sc[u["d"], u["h"]]
        lhs = jnp.concatenate([u["uw"][:, DN_DV:].astype(BF16), u["qd"]], axis=0)
        u["ws"] = _dot(lhs, u["s"].astype(BF16))
    outs = [[], []]
    for u in units:
        v_new = (u["uw"][:, :DN_DV] - u["ws"][:CHUNK]).astype(BF16)
        if need_out:
            outs[u["d"]].append(u["ws"][CHUNK:] + _dot(u["attn"], v_new))
        s_sc[u["d"], u["h"]] = u["s"] * u["a_end"] + _dot_tn(u["kend"], v_new)

    if need_out:
        of_ref[...] = jnp.concatenate(outs[0], axis=1)
        ob_ref[...] = jnp.concatenate(outs[1], axis=1)

    @pl.when(c == pl.num_programs(1) - 1)
    def _():
        sf_ref[...] = s_sc[...]


def _delta_rule(qkv, small, par, e_beta, e_gate, s0, *, batch, need_out):
    n = qkv.shape[0]
    n_chunks = n // batch // CHUNK
    fwd = lambda b, c: b * n_chunks + c
    bwd = lambda b, c: b * n_chunks + (n_chunks - 1 - c)
    in_specs = []
    for row in (fwd, bwd):
        in_specs += [
            pl.BlockSpec((CHUNK, QK_W), lambda b, c, row=row: (row(b, c), 0)),
            pl.BlockSpec((CHUNK, QK_W), lambda b, c, row=row: (row(b, c), 1)),
            pl.BlockSpec((CHUNK, DN_WIDTH), lambda b, c, row=row: (row(b, c), 2)),
            pl.BlockSpec((CHUNK, BD_PAD), lambda b, c, row=row: (row(b, c), 0)),
        ]
    const = lambda shape: pl.BlockSpec(shape, lambda b, c: (0, 0))
    state_spec = pl.BlockSpec((2, None, DN_HEADS, DN_DK, DN_DV), lambda b, c: (0, b, 0, 0, 0))
    in_specs += [const((8, LANE)), const((LANE, DN_WIDTH)), const((LANE, DN_WIDTH)),
                 const((LANE, DN_WIDTH)), const((LANE, DN_WIDTH)), state_spec]
    state_shape = jax.ShapeDtypeStruct((2, batch, DN_HEADS, DN_DK, DN_DV), F32)
    if need_out:
        out_specs = [pl.BlockSpec((CHUNK, DN_WIDTH), lambda b, c: (fwd(b, c), 0)),
                     pl.BlockSpec((CHUNK, DN_WIDTH), lambda b, c: (bwd(b, c), 0)), state_spec]
        out_shape = [jax.ShapeDtypeStruct((n, DN_WIDTH), F32)] * 2 + [state_shape]
    else:
        out_specs = [state_spec]
        out_shape = [state_shape]
    res = pl.pallas_call(
        functools.partial(_delta_kernel, need_out=need_out),
        grid=(batch, n_chunks),
        in_specs=in_specs,
        out_specs=out_specs,
        out_shape=out_shape,
        scratch_shapes=[pltpu.VMEM((2, DN_HEADS, DN_DK, DN_DV), F32)],
        compiler_params=pltpu.CompilerParams(dimension_semantics=("arbitrary", "arbitrary"),
                                             vmem_limit_bytes=VMEM_LIMIT),
        name="delta_out" if need_out else "delta_state",
    )(qkv, qkv, qkv, small, qkv, qkv, qkv, small, par, e_beta[0], e_gate[0], e_beta[1], e_gate[1], s0)
    return res if need_out else (None, None, res[0])


def _cmul_add(a_re, a_im, h_re, h_im, x_re, x_im):
    return a_re * h_re - a_im * h_im + x_re, a_re * h_im + a_im * h_re + x_im


def _s5_kernel(*refs, reverse, need_out):
    n_u = S5_WIDTH // LANE
    u_refs, refs = refs[:n_u], refs[n_u:]
    if need_out:
        b_ref, c_ref, pr_ref, pi_ref, h0_ref, y_ref, hf_ref, up_sc, x_sc, h_sc, y_sc, carry_sc = refs
    else:
        b_ref, pr_ref, pi_ref, h0_ref, hf_ref, up_sc, x_sc, carry_sc = refs
    blk = pl.program_id(1)

    @pl.when(blk == 0)
    def _():
        carry_sc[...] = h0_ref[...]

    for j in range(S5_STEPS):
        for k in range(n_u):
            up_sc[j * S5_SEG:(j + 1) * S5_SEG, k * LANE:(k + 1) * LANE] = (
                u_refs[k][pl.ds(j, S5_SEG, stride=S5_STEPS), :])
    u = up_sc[...].astype(BF16)
    half_u = S5_WIDTH // 2
    for hf in range(2):
        x_sc[:, hf * 2 * S5_HALF_STATE:(hf + 1) * 2 * S5_HALF_STATE] = _dot(
            u[:, hf * half_u:(hf + 1) * half_u], b_ref[hf])

    order = list(range(S5_STEPS - 1, -1, -1)) if reverse else list(range(S5_STEPS))
    seg = lax.broadcasted_iota(jnp.int32, (S5_SEG, LANE), 0)
    first_seg = S5_SEG - 1 if reverse else 0
    last_seg = 0 if reverse else S5_SEG - 1
    n_chunk = S5_HALF_STATE // LANE

    def scan_chunk(off_re, off_im):
        re_l, im_l = pl.ds(off_re, LANE), pl.ds(off_im, LANE)
        a_re, a_im = pr_ref[0:1, re_l], pi_ref[0:1, re_l]
        h_re = jnp.zeros((S5_SEG, LANE), F32)
        h_im = jnp.zeros((S5_SEG, LANE), F32)
        for j in order:
            rows = pl.ds(j * S5_SEG, S5_SEG)
            h_re, h_im = _cmul_add(a_re, a_im, h_re, h_im, x_sc[rows, re_l], x_sc[rows, im_l])
        toward = 1 if not reverse else S5_SEG - 1
        f_re = jnp.where(seg == first_seg, carry_sc[0:1, re_l], pltpu.roll(h_re, toward, axis=0))
        f_im = jnp.where(seg == first_seg, carry_sc[0:1, im_l], pltpu.roll(h_im, toward, axis=0))
        for k in range(S5_SEG_LOG):
            d = 1 << k
            valid = (seg < S5_SEG - d) if reverse else (seg >= d)
            amount = S5_SEG - d if reverse else d
            p_re = jnp.where(valid, pltpu.roll(f_re, amount, axis=0), 0.0)
            p_im = jnp.where(valid, pltpu.roll(f_im, amount, axis=0), 0.0)
            f_re, f_im = _cmul_add(pr_ref[1 + k:2 + k, re_l], pi_ref[1 + k:2 + k, re_l], p_re, p_im, f_re, f_im)
        n_re, n_im = _cmul_add(pr_ref[1:2, re_l], pi_ref[1:2, re_l], f_re, f_im, h_re, h_im)
        carry_sc[0:1, re_l] = n_re[last_seg:last_seg + 1, :]
        carry_sc[0:1, im_l] = n_im[last_seg:last_seg + 1, :]
        if need_out:
            h_re, h_im = f_re, f_im
            for j0 in range(0, S5_STEPS, 2):
                pair = {}
                for j in order[j0:j0 + 2]:
                    rows = pl.ds(j * S5_SEG, S5_SEG)
                    h_re, h_im = _cmul_add(a_re, a_im, h_re, h_im, x_sc[rows, re_l], x_sc[rows, im_l])
                    pair[j] = (h_re, h_im)
                lo = min(pair)
                rows2 = pl.ds(lo * S5_SEG, 2 * S5_SEG)
                h_sc[rows2, re_l] = jnp.concatenate([pair[lo][0], pair[lo + 1][0]], axis=0).astype(BF16)
                h_sc[rows2, im_l] = jnp.concatenate([pair[lo][1], pair[lo + 1][1]], axis=0).astype(BF16)

    def chunk_group(i, _):
        for r in range(S5_UNROLL):
            idx = i * S5_UNROLL + r
            hf = idx // n_chunk
            off_re = pl.multiple_of(hf * (2 * S5_HALF_STATE) + (idx % n_chunk) * LANE, LANE)
            scan_chunk(off_re, pl.multiple_of(off_re + S5_HALF_STATE, LANE))
        return 0

    lax.fori_loop(0, 2 * n_chunk // S5_UNROLL, chunk_group, 0)

    if need_out:
        for hf in range(2):
            y_half = _dot(h_sc[:, hf * 2 * S5_HALF_STATE:(hf + 1) * 2 * S5_HALF_STATE], c_ref[hf])
            for k in range(half_u // LANE):
                y_sc[hf * (half_u // LANE) + k] = y_half[:, k * LANE:(k + 1) * LANE]
        for k in range(n_u):
            for s in range(S5_SEG):
                y_ref[s * S5_STEPS:(s + 1) * S5_STEPS, k * LANE:(k + 1) * LANE] = (
                    y_sc[k, pl.ds(s, S5_STEPS, stride=S5_SEG), :])

    @pl.when(blk == pl.num_programs(1) - 1)
    def _():
        hf_ref[...] = carry_sc[...]


def _s5_scan(small, b_blk, c_blk, tabs, h0, *, batch, direction, need_out):
    n = small.shape[0]
    length = n // batch
    tb = S5_BLOCK
    n_blocks = length // tb
    reverse = direction == 1
    if reverse:
        row = lambda b, i: b * n_blocks + (n_blocks - 1 - i)
    else:
        row = lambda b, i: b * n_blocks + i
    const2 = lambda shape: pl.BlockSpec(shape, lambda b, i: (0,) * len(shape))
    n_u = S5_WIDTH // LANE
    u_spec = lambda k: pl.BlockSpec((tb, LANE), lambda b, i: (row(b, i), BD_PAD // LANE + k))
    in_specs = [u_spec(k) for k in range(n_u)] + [const2(b_blk.shape)]
    args = [small] * n_u + [b_blk]
    if need_out:
        in_specs.append(const2(c_blk.shape))
        args.append(c_blk)
    in_specs += [const2(t.shape) for t in tabs]
    in_specs.append(pl.BlockSpec((None, 8, S5_LANES), lambda b, i: (b, 0, 0)))
    args += list(tabs) + [h0]
    carry_spec = pl.BlockSpec((None, 8, S5_LANES), lambda b, i: (b, 0, 0))
    carry_shape = jax.ShapeDtypeStruct((batch, 8, S5_LANES), F32)
    scratch = [pltpu.VMEM((tb, S5_WIDTH), F32), pltpu.VMEM((tb, S5_LANES), F32)]
    if need_out:
        out_specs = [pl.BlockSpec((tb, S5_WIDTH), lambda b, i: (row(b, i), 0)), carry_spec]
        out_shape = [jax.ShapeDtypeStruct((n, S5_WIDTH), F32), carry_shape]
        scratch += [pltpu.VMEM((tb, S5_LANES), BF16), pltpu.VMEM((n_u, tb, LANE), F32)]
    else:
        out_specs = [carry_spec]
        out_shape = [carry_shape]
    scratch.append(pltpu.VMEM((8, S5_LANES), F32))
    res = pl.pallas_call(
        functools.partial(_s5_kernel, reverse=reverse, need_out=need_out),
        grid=(batch, n_blocks),
        in_specs=in_specs,
        out_specs=out_specs,
        out_shape=out_shape,
        scratch_shapes=scratch,
        compiler_params=pltpu.CompilerParams(dimension_semantics=("arbitrary", "arbitrary"),
                                             vmem_limit_bytes=VMEM_LIMIT),
        name=f"s5_d{direction}" + ("_out" if need_out else "_state"),
    )(*args)
    return res if need_out else (None, res[0])


def _out_kernel(x_ref, gate_ref, of_ref, ob_ref, dnw_ref, szdn_ref, sgdn_ref, sgs5_ref, szs5_ref,
                yf_ref, yb_ref, small_ref, d_ref, gluw_ref, glub_ref, wpd_ref, wps_ref, wo_ref, fnw_ref,
                o_ref):
    o = of_ref[...] + ob_ref[...]
    parts = []
    for h in range(DN_HEADS):
        t = o[:, h * DN_DV:(h + 1) * DN_DV]
        parts.append(t * lax.rsqrt(jnp.mean(t * t, axis=-1, keepdims=True) + NORM_EPS))
    y_dn = jnp.concatenate(parts, axis=1) * dnw_ref[...] * szdn_ref[...]
    u = small_ref[:, BD_PAD:BD_PAD + S5_WIDTH]
    y = d_ref[...] * u + yf_ref[...] + yb_ref[...]
    y = 0.5 * y * (1.0 + jnp.tanh(0.7978845608028654 * (y + 0.044715 * (y * y * y))))
    y = y * _sigmoid(_dot(y.astype(BF16), gluw_ref[...]) + glub_ref[...])
    y_s5 = y * szs5_ref[...]
    merged = (sgdn_ref[...] * _dot(y_dn.astype(BF16), wpd_ref[...])
              + sgs5_ref[...] * _dot(y_s5.astype(BF16), wps_ref[...]))
    out = _dot(merged.astype(BF16), wo_ref[...])
    xo = x_ref[...] + gate_ref[...] * out
    o_ref[...] = xo * lax.rsqrt(jnp.mean(xo * xo, axis=-1, keepdims=True) + NORM_EPS) * fnw_ref[...]


def _output_stage(x2, mod3, o_f, o_b, dn_norm_w, gates, y_f, y_b, small, s5_d, glu_w, glu_b,
                  w_proj_dn, w_proj_s5, w_out, final_norm_w, *, batch, tm):
    n = x2.shape[0]
    tiles_per_batch = n // batch // tm
    tok = lambda width, col: pl.BlockSpec((tm, width), lambda i: (i, col))
    const = lambda shape: pl.BlockSpec(shape, lambda i: (0, 0))
    in_specs = [
        tok(D_MODEL, 0),
        pl.BlockSpec((None, 1, D_MODEL), lambda i: (i // tiles_per_batch, 0, 2)),
        tok(DN_WIDTH, 0), tok(DN_WIDTH, 0), const((1, DN_WIDTH)),
        tok(D_MODEL, 0), tok(D_MODEL, 1), tok(D_MODEL, 2), tok(S5_WIDTH, 3 * D_MODEL // S5_WIDTH),
        tok(S5_WIDTH, 0), tok(S5_WIDTH, 0), tok(SMALL_W, 0), const((1, S5_WIDTH)),
        const((S5_WIDTH, S5_WIDTH)), const((1, S5_WIDTH)),
        const((DN_WIDTH, D_MODEL)), const((S5_WIDTH, D_MODEL)), const((D_MODEL, D_MODEL)),
        const((1, D_MODEL)),
    ]
    return pl.pallas_call(
        _out_kernel,
        grid=(n // tm,),
        in_specs=in_specs,
        out_specs=pl.BlockSpec((tm, D_MODEL), lambda i: (i, 0)),
        out_shape=jax.ShapeDtypeStruct((n, D_MODEL), F32),
        compiler_params=pltpu.CompilerParams(dimension_semantics=("arbitrary",),
                                             vmem_limit_bytes=VMEM_LIMIT),
        name="output_stage",
    )(x2, mod3, o_f, o_b, dn_norm_w, gates, gates, gates, gates, y_f, y_b, small, s5_d,
      glu_w, glu_b, w_proj_dn, w_proj_s5, w_out, final_norm_w)


def _gate_lane_params(a_log, dt_bias):
    par = jnp.zeros((8, LANE), F32)
    par = par.at[0, 2 * DN_HEADS:4 * DN_HEADS].set(a_log.reshape(-1).astype(F32))
    par = par.at[1, 2 * DN_HEADS:4 * DN_HEADS].set(dt_bias.reshape(-1).astype(F32))
    return par


def _expand_matrix(first_lane):
    lane = jnp.arange(LANE)[:, None]
    head = jnp.arange(DN_WIDTH)[None, :] // DN_DV
    return (lane == first_lane + head).astype(BF16)


def kernel(x, c, ctx, c_ctx, w_ada, b_ada, norm_w, w_in, conv_w, dn_A_log, dn_dt_bias, dn_norm_w,
           s5_lam_re, s5_lam_im, s5_log_step, s5_B_re, s5_B_im, s5_C_re, s5_C_im, s5_D, glu_w, glu_b,
           w_proj_dn, w_proj_s5, w_out, final_norm_w):
    batch, seq, d_model = x.shape
    ctx_len = ctx.shape[1]
    depth = w_in.shape[0]
    assert d_model == D_MODEL and depth == 1 and batch <= 4
    assert seq % 512 == 0 and ctx_len % S5_BLOCK == 0 and ctx_len & (ctx_len - 1) == 0
    i = 0
    x2 = x.reshape(batch * seq, D_MODEL)
    ctx2 = ctx.reshape(batch * ctx_len, D_MODEL)

    w_full = w_in[i]
    w_qkv = w_full[:, :QKV_W].astype(BF16)
    w_small = jnp.concatenate(
        [w_full[:, O_BETA:O_U], jnp.zeros((D_MODEL, BD_PAD - (O_U - O_BETA)), w_full.dtype),
         w_full[:, O_U:STATE_COLS]], axis=1).astype(BF16)
    w_gate = jnp.concatenate(
        [w_full[:, O_ZDN:O_ZS5], w_full[:, O_GDN:O_GS5], w_full[:, O_GS5:IN_WIDTH],
         w_full[:, O_ZS5:O_GDN]], axis=1).astype(BF16)
    conv_w8 = jnp.concatenate([conv_w[i], jnp.zeros((8 - CONV_K, QKV_W), conv_w.dtype)], axis=0).astype(F32)
    nw = norm_w[i].reshape(1, D_MODEL).astype(F32)
    e_beta = [_expand_matrix(DN_HEADS * d) for d in range(2)]
    e_gate = [_expand_matrix(2 * DN_HEADS + DN_HEADS * d) for d in range(2)]
    par = _gate_lane_params(dn_A_log[i], dn_dt_bias[i])

    cvec = jnp.zeros((8, D_MODEL), F32).at[:batch].set(c.astype(F32)).at[batch].set(c_ctx.astype(F32))
    mod = _mod_vectors(cvec, w_ada[i].astype(F32), b_ada[i].reshape(1, -1).astype(F32))
    mod3 = mod.reshape(8, 1, 3 * D_MODEL)

    pow_re, pow_im, bbar_re, bbar_im = _s5_prepare(
        s5_lam_re[i], s5_lam_im[i], s5_log_step[i], s5_B_re[i], s5_B_im[i])
    b_blk, c_blk = _s5_block_matrices(
        bbar_re, bbar_im,
        s5_C_re[i].astype(F32), s5_C_im[i].astype(F32))
    pad8 = lambda t: jnp.concatenate([t, jnp.zeros((-t.shape[0] % 8,) + t.shape[1:], t.dtype)], axis=0)
    s5_tabs = [tuple(pad8(_s5_lane_layout(t, t)[:, d]) for t in (pow_re, pow_im)) for d in range(2)]

    ctx_row = lambda t: batch
    ctx_qkv = _inproj(ctx2, nw, mod3, ctx_row, w_qkv, conv_w8, epilogue="qkv", rows=ctx_len, tm=ctx_len)
    ctx_small = _inproj(ctx2, nw, mod3, ctx_row, w_small, None, epilogue="raw", rows=ctx_len, tm=ctx_len)
    zero_state = jnp.zeros((2, batch, DN_HEADS, DN_DK, DN_DV), F32)
    zero_h = jnp.zeros((batch, 8, S5_LANES), F32)
    _, _, s_ctx = _delta_rule(ctx_qkv, ctx_small, par, e_beta, e_gate, zero_state, batch=batch, need_out=False)
    h_ctx = []
    for d in range(2):
        _, h_fin = _s5_scan(ctx_small, b_blk[d], None, s5_tabs[d], zero_h,
                            batch=batch, direction=d, need_out=False)
        h_ctx.append(h_fin)

    tm = 512
    tiles_per_batch = seq // tm
    lat_row = lambda t: t // tiles_per_batch
    qkv = _inproj(x2, nw, mod3, lat_row, w_qkv, conv_w8, epilogue="qkv", rows=GRID_W, tm=tm)
    small = _inproj(x2, nw, mod3, lat_row, w_small, None, epilogue="raw", rows=GRID_W, tm=tm)
    gates = _inproj(x2, nw, mod3, lat_row, w_gate, None, epilogue="gate", rows=GRID_W, tm=tm)
    o_f, o_b, _ = _delta_rule(qkv, small, par, e_beta, e_gate, s_ctx, batch=batch, need_out=True)
    o_dir = [o_f, o_b]
    y_dir = []
    for d in range(2):
        y_d, _ = _s5_scan(small, b_blk[d], c_blk[d], s5_tabs[d], h_ctx[d],
                          batch=batch, direction=d, need_out=True)
        y_dir.append(y_d)

    out = _output_stage(
        x2, mod3, o_dir[0], o_dir[1], jnp.tile(dn_norm_w[i].astype(F32), DN_HEADS).reshape(1, DN_WIDTH),
        gates, y_dir[0], y_dir[1], small, s5_D[i].reshape(1, S5_WIDTH).astype(F32),
        glu_w[i].astype(BF16), glu_b[i].reshape(1, S5_WIDTH).astype(F32),
        w_proj_dn[i].astype(BF16), w_proj_s5[i].astype(BF16), w_out[i].astype(BF16),
        final_norm_w.reshape(1, D_MODEL).astype(F32), batch=batch, tm=256)
    return out.reshape(batch, seq, D_MODEL)
```

```python
import functools

import jax
import jax.numpy as jnp
from jax import lax
from jax.experimental import pallas as pl
from jax.experimental.pallas import tpu as pltpu

F32 = jnp.float32
BF16 = jnp.bfloat16
HIGHEST = lax.Precision.HIGHEST

D_MODEL = 1024
NORM_EPS = 1e-6
GRID_W = 64
DN_HEADS = 8
DN_DK = 128
DN_DV = 128
DN_WIDTH = DN_HEADS * DN_DV
CONV_K = 5
CHUNK = 64
S5_WIDTH = 512
S5_GROUP = 16
S5_GROUPS = S5_WIDTH // S5_GROUP
S5_STATE = 64
S5_HALF_GROUPS = S5_GROUPS // 2
S5_HALF_STATE = S5_HALF_GROUPS * S5_STATE
S5_LANES = 4 * S5_HALF_STATE
S5_BLOCK = 256
S5_SEG = 8
S5_STEPS = S5_BLOCK // S5_SEG
S5_SEG_LOG = 3
S5_UNROLL = 4
QK_W = DN_HEADS * DN_DK
QKV_W = 2 * QK_W + DN_WIDTH
O_BETA = QKV_W
O_DECAY = O_BETA + 2 * DN_HEADS
O_U = O_DECAY + 2 * DN_HEADS
STATE_COLS = O_U + S5_WIDTH
O_ZDN = STATE_COLS
O_ZS5 = O_ZDN + DN_WIDTH
O_GDN = O_ZS5 + S5_WIDTH
O_GS5 = O_GDN + D_MODEL
IN_WIDTH = O_GS5 + D_MODEL

LANE = 128
BD_PAD = LANE
SMALL_W = BD_PAD + S5_WIDTH
GATE_W = 3 * D_MODEL + S5_WIDTH
COL_TILE = 512
VMEM_LIMIT = 48 * 1024 * 1024


def _sigmoid(x):
    return 0.5 * jnp.tanh(0.5 * x) + 0.5


def _silu(x):
    half = 0.5 * x
    return half * jnp.tanh(half) + half


def _softplus(x):
    return jnp.maximum(x, 0.0) + jnp.log1p(jnp.exp(-jnp.abs(x)))


def _dot(a, b):
    return jnp.dot(a, b, preferred_element_type=F32)


def _dot_hi(a, b):
    return jnp.dot(a, b, precision=HIGHEST, preferred_element_type=F32)


def _dot_nt(a, b):
    return lax.dot_general(a, b, (((1,), (1,)), ((), ())), preferred_element_type=F32)


def _dot_tn(a, b):
    return lax.dot_general(a, b, (((0,), (0,)), ((), ())), preferred_element_type=F32)


def _split3(x):
    hi = x.astype(BF16)
    r1 = x - hi.astype(F32)
    mid = r1.astype(BF16)
    lo = (r1 - mid.astype(F32)).astype(BF16)
    return hi, mid, lo


def _dot_sel_rhs(x, sel):
    hi, mid, lo = _split3(x)
    return _dot(hi, sel) + _dot(mid, sel) + _dot(lo, sel)


def _dot_sel_lhs(sel, x):
    hi, mid, lo = _split3(x)
    return _dot(sel, hi) + _dot(sel, mid) + _dot(sel, lo)


def _mod_kernel(c_ref, w_ref, b_ref, o_ref):
    c = c_ref[...]
    s = _silu(c)
    o_ref[...] = _dot_hi(s, w_ref[...]) + b_ref[...]


def _mod_vectors(cvec, w_ada, b_ada):
    n = w_ada.shape[1]
    return pl.pallas_call(
        _mod_kernel,
        grid=(n // COL_TILE,),
        in_specs=[
            pl.BlockSpec((8, D_MODEL), lambda j: (0, 0)),
            pl.BlockSpec((D_MODEL, COL_TILE), lambda j: (0, j)),
            pl.BlockSpec((1, COL_TILE), lambda j: (0, j)),
        ],
        out_specs=pl.BlockSpec((8, COL_TILE), lambda j: (0, j)),
        out_shape=jax.ShapeDtypeStruct((8, n), F32),
        compiler_params=pltpu.CompilerParams(dimension_semantics=("arbitrary",)),
        name="mod_vectors",
    )(cvec, w_ada, b_ada)


def _s5_prep_kernel(lre_ref, lim_ref, ls_ref, bre_ref, bim_ref, pr_ref, pi_ref, bbr_ref, bbi_ref):
    lam_re = lre_ref[...]
    lam_im = lim_ref[...]
    dt = jnp.exp(ls_ref[...])
    mag = jnp.exp(lam_re * dt)
    abar_re = mag * jnp.cos(lam_im * dt)
    abar_im = mag * jnp.sin(lam_im * dt)
    num_re = abar_re - 1.0
    num_im = abar_im
    den = lam_re * lam_re + lam_im * lam_im
    f_re = (num_re * lam_re + num_im * lam_im) / den
    f_im = (num_im * lam_re - num_re * lam_im) / den
    b_re = bre_ref[...]
    b_im = bim_ref[...]
    bbr_ref[...] = f_re * b_re - f_im * b_im
    bbi_ref[...] = f_re * b_im + f_im * b_re
    pr_ref[0] = abar_re
    pi_ref[0] = abar_im
    pr, pi = abar_re, abar_im
    n = 1
    while n < S5_STEPS:
        pr, pi = pr * pr - pi * pi, 2.0 * pr * pi
        n *= 2
    for k in range(S5_SEG_LOG):
        pr_ref[1 + k] = pr
        pi_ref[1 + k] = pi
        pr, pi = pr * pr - pi * pi, 2.0 * pr * pi


def _s5_prepare(lam_re, lam_im, log_step, b_re, b_im):
    dg = 2 * S5_GROUPS
    lre = lam_re.reshape(dg, 1, S5_STATE)
    lim = lam_im.reshape(dg, 1, S5_STATE)
    ls = jnp.broadcast_to(log_step.reshape(dg, 1, 1), (dg, 1, S5_STATE))
    bre = jnp.swapaxes(b_re.reshape(dg, S5_STATE, S5_GROUP), 1, 2)
    bim = jnp.swapaxes(b_im.reshape(dg, S5_STATE, S5_GROUP), 1, 2)
    full3 = lambda shape: pl.BlockSpec(shape, lambda: (0,) * len(shape))
    pow_shape = (1 + S5_SEG_LOG, dg, 1, S5_STATE)
    pr, pi, bbr, bbi = pl.pallas_call(
        _s5_prep_kernel,
        in_specs=[full3(lre.shape), full3(lim.shape), full3(ls.shape), full3(bre.shape), full3(bim.shape)],
        out_specs=[full3(pow_shape), full3(pow_shape), full3(bre.shape), full3(bim.shape)],
        out_shape=[jax.ShapeDtypeStruct(pow_shape, F32), jax.ShapeDtypeStruct(pow_shape, F32),
                   jax.ShapeDtypeStruct(bre.shape, F32), jax.ShapeDtypeStruct(bim.shape, F32)],
        name="s5_discretise",
    )(lre.astype(F32), lim.astype(F32), ls.astype(F32), bre.astype(F32), bim.astype(F32))
    per_dir = lambda t: t.reshape(t.shape[0], 2, S5_GROUPS, S5_STATE)
    bbar_re = bbr.reshape(2, S5_GROUPS, S5_GROUP, S5_STATE)
    bbar_im = bbi.reshape(2, S5_GROUPS, S5_GROUP, S5_STATE)
    return per_dir(pr), per_dir(pi), bbar_re, bbar_im


def _s5_lane_layout(re, im):
    lead = re.shape[:-2]
    re = re.reshape(*lead, 2, S5_HALF_STATE)
    im = im.reshape(*lead, 2, S5_HALF_STATE)
    return jnp.concatenate([re, im], axis=-1).reshape(*lead, S5_LANES)


def _s5_block_matrices(bbar_re, bbar_im, c_re, c_im):
    eye = jnp.eye(S5_HALF_GROUPS, dtype=F32)

    def in_map(bb):
        bb = bb.reshape(2, 2, S5_HALF_GROUPS, S5_GROUP, S5_STATE)
        m = jnp.einsum('dhgcp,gk->dhgckp', bb, eye)
        return m.reshape(2, 2, S5_HALF_GROUPS * S5_GROUP, S5_HALF_STATE)

    def out_map(cc):
        cc = cc.reshape(2, 2, S5_HALF_GROUPS, S5_GROUP, S5_STATE)
        m = jnp.einsum('dhgcp,gk->dhgpkc', cc, eye)
        return m.reshape(2, 2, S5_HALF_STATE, S5_HALF_GROUPS * S5_GROUP)

    b_blk = jnp.concatenate([in_map(bbar_re), in_map(bbar_im)], axis=-1)
    c_blk = jnp.concatenate([out_map(c_re), out_map(-c_im)], axis=-2)
    return b_blk.astype(BF16), c_blk.astype(BF16)


def _conv_silu(acc, cw, rows):
    tm = acc.shape[0]
    pos = lax.broadcasted_iota(jnp.int32, acc.shape, 0) & (rows - 1)
    half = CONV_K // 2
    y = acc * cw[half:half + 1, :]
    for j in range(CONV_K):
        s = j - half
        if s == 0:
            continue
        shifted = pltpu.roll(acc, (-s) % tm, axis=0)
        valid = (pos >= -s) if s < 0 else (pos < rows - s)
        y = y + jnp.where(valid, shifted, 0.0) * cw[j:j + 1, :]
    return _silu(y)


def _l2norm_heads(y, scale):
    outs = []
    for h in range(y.shape[1] // DN_DK):
        t = y[:, h * DN_DK:(h + 1) * DN_DK]
        outs.append(t * (lax.rsqrt(jnp.sum(t * t, axis=-1, keepdims=True) + NORM_EPS) * scale))
    return jnp.concatenate(outs, axis=1)


def _inproj_kernel(*refs, epilogue, rows):
    if epilogue == "qkv":
        x_ref, nw_ref, sc_ref, sh_ref, w_ref, cw_ref, o_ref, h_sc = refs
    else:
        x_ref, nw_ref, sc_ref, sh_ref, w_ref, o_ref, h_sc = refs
    j = pl.program_id(1)

    @pl.when(j == 0)
    def _():
        x = x_ref[...]
        y = x * lax.rsqrt(jnp.mean(x * x, axis=-1, keepdims=True) + NORM_EPS) * nw_ref[...]
        h_sc[...] = (y * (1.0 + sc_ref[...]) + sh_ref[...]).astype(BF16)

    acc = _dot(h_sc[...], w_ref[...])
    if epilogue == "raw":
        o_ref[...] = acc
    elif epilogue == "gate":
        is_silu = jnp.logical_or(j < 2, j == 6)

        @pl.when(is_silu)
        def _():
            o_ref[...] = _silu(acc)

        @pl.when(jnp.logical_not(is_silu))
        def _():
            o_ref[...] = _sigmoid(acc)
    else:
        y = _conv_silu(acc, cw_ref[...], rows)
        n_qk_tiles = QK_W // COL_TILE

        @pl.when(j < 2 * n_qk_tiles)
        def _():
            scale = jnp.where(j < n_qk_tiles, DN_DK ** -0.5, 1.0)
            o_ref[...] = _l2norm_heads(y, scale)

        @pl.when(j >= 2 * n_qk_tiles)
        def _():
            o_ref[...] = y


def _inproj(x2, norm_w, mod3, mod_row, w, conv_w, *, epilogue, rows, tm):
    n = x2.shape[0]
    width = w.shape[1]
    tn = COL_TILE if width % COL_TILE == 0 else width
    grid = (n // tm, width // tn)
    vec = lambda k: pl.BlockSpec((None, 1, D_MODEL), lambda i, j: (mod_row(i), 0, k))
    in_specs = [
        pl.BlockSpec((tm, D_MODEL), lambda i, j: (i, 0)),
        pl.BlockSpec((1, D_MODEL), lambda i, j: (0, 0)),
        vec(1),
        vec(0),
        pl.BlockSpec((D_MODEL, tn), lambda i, j: (0, j)),
    ]
    args = [x2, norm_w, mod3, mod3, w]
    if epilogue == "qkv":
        in_specs.append(pl.BlockSpec((8, tn), lambda i, j: (0, j)))
        args.append(conv_w)
    return pl.pallas_call(
        functools.partial(_inproj_kernel, epilogue=epilogue, rows=rows),
        grid=grid,
        in_specs=in_specs,
        out_specs=pl.BlockSpec((tm, tn), lambda i, j: (i, j)),
        out_shape=jax.ShapeDtypeStruct((n, width), F32),
        scratch_shapes=[pltpu.VMEM((tm, D_MODEL), BF16)],
        compiler_params=pltpu.CompilerParams(dimension_semantics=("arbitrary", "arbitrary"),
                                             vmem_limit_bytes=VMEM_LIMIT),
        name="inproj_" + epilogue,
    )(*args)


def _block_masks(n, reverse):
    ri = lax.broadcasted_iota(jnp.int32, (n, n), 0)
    ci = lax.broadcasted_iota(jnp.int32, (n, n), 1)
    lo, hi = (ri, ci) if reverse else (ci, ri)
    masks = []
    b = 1
    while b < n:
        masks.append(((hi ^ lo) < 2 * b) & ((hi & b) != 0) & ((lo & b) == 0))
        b *= 2
    return masks


def _delta_chunk_prep(q, k, v, bd, par, e_beta, e_gate, direction):
    reverse = direction == 1
    beta_all = _sigmoid(bd)
    g_all = -jnp.exp(par[0:1, :]) * _softplus(bd + par[1:2, :])
    ri = lax.broadcasted_iota(jnp.int32, (CHUNK, CHUNK), 0)
    ci = lax.broadcasted_iota(jnp.int32, (CHUNK, CHUNK), 1)
    incl = (ri <= ci) if reverse else (ri >= ci)
    strict = (ri < ci) if reverse else (ri > ci)
    tri = jnp.where(incl, 1.0, 0.0).astype(BF16)

    g_cum = _dot_sel_lhs(tri, g_all)
    g_cum_t = jnp.transpose(jnp.concatenate([g_cum, jnp.zeros_like(g_cum)], axis=0))
    beta_x = _dot(beta_all.astype(BF16), e_beta)
    g_hi = g_cum.astype(BF16)
    g_x = _dot(g_hi, e_gate) + _dot((g_cum - g_hi.astype(F32)).astype(BF16), e_gate)
    edge = 0 if reverse else CHUNK - 1
    g_last = g_x[edge:edge + 1, :]
    e_g = jnp.exp(g_x)
    a_end = jnp.exp(g_last)
    kb = k * beta_x
    k_bf = k.astype(BF16)
    kq = jnp.concatenate([kb, q], axis=0).astype(BF16)
    vb = (v * beta_x).astype(BF16)
    kbe = (kb * e_g).astype(BF16)
    qd = (q * e_g).astype(BF16)
    kend = (k * jnp.exp(g_last - g_x)).astype(BF16)

    heads = []
    for h in range(DN_HEADS):
        sl = slice(h * DN_DK, (h + 1) * DN_DK)
        gate_lane = 2 * DN_HEADS + DN_HEADS * direction + h
        diff = g_x[:, h * DN_DK:h * DN_DK + CHUNK] - g_cum_t[gate_lane:gate_lane + 1, 0:CHUNK]
        heads.append(dict(
            dec=jnp.exp(jnp.where(incl, diff, -jnp.inf)),
            kq=kq[:, sl], k=k_bf[:, sl],
            rhs=jnp.concatenate([vb[:, sl], kbe[:, sl]], axis=1),
            qd=qd[:, sl], kend=kend[:, sl], a_end=a_end[:, sl]))
    return heads, strict, _block_masks(CHUNK, reverse)


def _delta_kernel(*refs, need_out):
    ins, rest = refs[:14], refs[14:]
    (qf_ref, kf_ref, vf_ref, bdf_ref, qb_ref, kb_ref, vb_ref, bdb_ref,
     par_ref, eb0_ref, eg0_ref, eb1_ref, eg1_ref, s0_ref) = ins
    if need_out:
        of_ref, ob_ref, sf_ref, s_sc = rest
    else:
        sf_ref, s_sc = rest
    c = pl.program_id(0)
    batch = qf_ref.shape[0]

    @pl.when(c == 0)
    def _():
        s_sc[...] = s0_ref[...]

    par = par_ref[...]
    units = []
    for b in range(batch):
        for d, (q_ref, k_ref, v_ref, bd_ref, eb_ref, eg_ref) in enumerate(
                ((qf_ref, kf_ref, vf_ref, bdf_ref, eb0_ref, eg0_ref),
                 (qb_ref, kb_ref, vb_ref, bdb_ref, eb1_ref, eg1_ref))):
            heads, strict, masks = _delta_chunk_prep(q_ref[b], k_ref[b], v_ref[b], bd_ref[b], par,
                                                     eb_ref[...], eg_ref[...], d)
            for h, unit in enumerate(heads):
                unit.update(b=b, d=d, h=h, strict=strict, masks=masks)
                units.append(unit)

    ri = lax.broadcasted_iota(jnp.int32, (CHUNK, CHUNK), 0)
    ci = lax.broadcasted_iota(jnp.int32, (CHUNK, CHUNK), 1)
    eye = jnp.where(ri == ci, 1.0, 0.0).astype(F32)
    for u in units:
        aq = _dot_nt(u["kq"], u["k"])
        u["am"] = jnp.where(u["strict"], aq[:CHUNK] * u["dec"], 0.0)
        u["attn"] = (aq[CHUNK:] * u["dec"]).astype(BF16)
        u["inv"] = eye - jnp.where(u["masks"][0], u["am"], 0.0)
    for level in range(1, len(units[0]["masks"])):
        for u in units:
            e = jnp.where(u["masks"][level], u["am"], 0.0).astype(BF16)
            u["inv_bf"] = u["inv"].astype(BF16)
            u["t"] = _dot(e, u["inv_bf"]).astype(BF16)
        for u in units:
            u["inv"] = u["inv"] - _dot(u["inv_bf"], u["t"])
    for u in units:
        u["uw"] = _dot(u["inv"].astype(BF16), u["rhs"])
    for u in units:
        u["s"] = s_sc[u["d"], u["b"], u["h"]]
        lhs = jnp.concatenate([u["uw"][:, DN_DV:].astype(BF16), u["qd"]], axis=0)
        u["ws"] = _dot(lhs, u["s"].astype(BF16))
    outs = {}
    for u in units:
        v_new = (u["uw"][:, :DN_DV] - u["ws"][:CHUNK]).astype(BF16)
        if need_out:
            outs.setdefault((u["d"], u["b"]), []).append(u["ws"][CHUNK:] + _dot(u["attn"], v_new))
        s_sc[u["d"], u["b"], u["h"]] = u["s"] * u["a_end"] + _dot_tn(u["kend"], v_new)

    if need_out:
        for b in range(batch):
            of_ref[b] = jnp.concatenate(outs[(0, b)], axis=1)
            ob_ref[b] = jnp.concatenate(outs[(1, b)], axis=1)

    @pl.when(c == pl.num_programs(0) - 1)
    def _():
        sf_ref[...] = s_sc[...]


def _delta_rule(qkv, small, par, e_beta, e_gate, s0, *, batch, need_out):
    n = qkv.shape[0]
    length = n // batch
    n_chunks = length // CHUNK
    qkv3 = qkv.reshape(batch, length, QKV_W)
    small3 = small.reshape(batch, length, SMALL_W)
    fwd = lambda c: c
    bwd = lambda c: n_chunks - 1 - c
    in_specs = []
    for row in (fwd, bwd):
        in_specs += [
            pl.BlockSpec((batch, CHUNK, QK_W), lambda c, row=row: (0, row(c), 0)),
            pl.BlockSpec((batch, CHUNK, QK_W), lambda c, row=row: (0, row(c), 1)),
            pl.BlockSpec((batch, CHUNK, DN_WIDTH), lambda c, row=row: (0, row(c), 2)),
            pl.BlockSpec((batch, CHUNK, BD_PAD), lambda c, row=row: (0, row(c), 0)),
        ]
    const = lambda shape: pl.BlockSpec(shape, lambda c: (0,) * len(shape))
    state_dims = (2, batch, DN_HEADS, DN_DK, DN_DV)
    in_specs += [const((8, LANE)), const((LANE, DN_WIDTH)), const((LANE, DN_WIDTH)),
                 const((LANE, DN_WIDTH)), const((LANE, DN_WIDTH)), const(state_dims)]
    state_shape = jax.ShapeDtypeStruct(state_dims, F32)
    if need_out:
        out_specs = [pl.BlockSpec((batch, CHUNK, DN_WIDTH), lambda c: (0, fwd(c), 0)),
                     pl.BlockSpec((batch, CHUNK, DN_WIDTH), lambda c: (0, bwd(c), 0)), const(state_dims)]
        out_shape = [jax.ShapeDtypeStruct((batch, length, DN_WIDTH), F32)] * 2 + [state_shape]
    else:
        out_specs = [const(state_dims)]
        out_shape = [state_shape]
    res = pl.pallas_call(
        functools.partial(_delta_kernel, need_out=need_out),
        grid=(n_chunks,),
        in_specs=in_specs,
        out_specs=out_specs,
        out_shape=out_shape,
        scratch_shapes=[pltpu.VMEM(state_dims, F32)],
        compiler_params=pltpu.CompilerParams(dimension_semantics=("arbitrary",),
                                             vmem_limit_bytes=VMEM_LIMIT),
        name="delta_out" if need_out else "delta_state",
    )(qkv3, qkv3, qkv3, small3, qkv3, qkv3, qkv3, small3, par,
      e_beta[0], e_gate[0], e_beta[1], e_gate[1], s0)
    if not need_out:
        return None, None, res[0]
    return res[0].reshape(n, DN_WIDTH), res[1].reshape(n, DN_WIDTH), res[2]


def _cmul_add(a_re, a_im, h_re, h_im, x_re, x_im):
    return a_re * h_re - a_im * h_im + x_re, a_re * h_im + a_im * h_re + x_im


def _s5_kernel(*refs, reverse, need_out):
    n_u = S5_WIDTH // LANE
    u_refs, refs = refs[:n_u], refs[n_u:]
    if need_out:
        b_ref, c_ref, pr_ref, pi_ref, h0_ref, y_ref, hf_ref, up_sc, x_sc, h_sc, y_sc, carry_sc = refs
    else:
        b_ref, pr_ref, pi_ref, h0_ref, hf_ref, up_sc, x_sc, carry_sc = refs
    blk = pl.program_id(1)

    @pl.when(blk == 0)
    def _():
        carry_sc[...] = h0_ref[...]

    for j in range(S5_STEPS):
        for k in range(n_u):
            up_sc[j * S5_SEG:(j + 1) * S5_SEG, k * LANE:(k + 1) * LANE] = (
                u_refs[k][pl.ds(j, S5_SEG, stride=S5_STEPS), :])
    u = up_sc[...].astype(BF16)
    half_u = S5_WIDTH // 2
    for hf in range(2):
        x_sc[:, hf * 2 * S5_HALF_STATE:(hf + 1) * 2 * S5_HALF_STATE] = _dot(
            u[:, hf * half_u:(hf + 1) * half_u], b_ref[hf])

    order = list(range(S5_STEPS - 1, -1, -1)) if reverse else list(range(S5_STEPS))
    seg = lax.broadcasted_iota(jnp.int32, (S5_SEG, LANE), 0)
    first_seg = S5_SEG - 1 if reverse else 0
    last_seg = 0 if reverse else S5_SEG - 1
    n_chunk = S5_HALF_STATE // LANE

    def scan_chunk(off_re, off_im):
        re_l, im_l = pl.ds(off_re, LANE), pl.ds(off_im, LANE)
        a_re, a_im = pr_ref[0:1, re_l], pi_ref[0:1, re_l]
        h_re = jnp.zeros((S5_SEG, LANE), F32)
        h_im = jnp.zeros((S5_SEG, LANE), F32)
        for j in order:
            rows = pl.ds(j * S5_SEG, S5_SEG)
            h_re, h_im = _cmul_add(a_re, a_im, h_re, h_im, x_sc[rows, re_l], x_sc[rows, im_l])
        toward = 1 if not reverse else S5_SEG - 1
        f_re = jnp.where(seg == first_seg, carry_sc[0:1, re_l], pltpu.roll(h_re, toward, axis=0))
        f_im = jnp.where(seg == first_seg, carry_sc[0:1, im_l], pltpu.roll(h_im, toward, axis=0))
        for k in range(S5_SEG_LOG):
            d = 1 << k
            valid = (seg < S5_SEG - d) if reverse else (seg >= d)
            amount = S5_SEG - d if reverse else d
            p_re = jnp.where(valid, pltpu.roll(f_re, amount, axis=0), 0.0)
            p_im = jnp.where(valid, pltpu.roll(f_im, amount, axis=0), 0.0)
            f_re, f_im = _cmul_add(pr_ref[1 + k:2 + k, re_l], pi_ref[1 + k:2 + k, re_l], p_re, p_im, f_re, f_im)
        n_re, n_im = _cmul_add(pr_ref[1:2, re_l], pi_ref[1:2, re_l], f_re, f_im, h_re, h_im)
        carry_sc[0:1, re_l] = n_re[last_seg:last_seg + 1, :]
        carry_sc[0:1, im_l] = n_im[last_seg:last_seg + 1, :]
        if need_out:
            h_re, h_im = f_re, f_im
            for j0 in range(0, S5_STEPS, 2):
                pair = {}
                for j in order[j0:j0 + 2]:
                    rows = pl.ds(j * S5_SEG, S5_SEG)
                    h_re, h_im = _cmul_add(a_re, a_im, h_re, h_im, x_sc[rows, re_l], x_sc[rows, im_l])
                    pair[j] = (h_re, h_im)
                lo = min(pair)
                rows2 = pl.ds(lo * S5_SEG, 2 * S5_SEG)
                h_sc[rows2, re_l] = jnp.concatenate([pair[lo][0], pair[lo + 1][0]], axis=0).astype(BF16)
                h_sc[rows2, im_l] = jnp.concatenate([pair[lo][1], pair[lo + 1][1]], axis=0).astype(BF16)

    def chunk_group(i, _):
        for r in range(S5_UNROLL):
            idx = i * S5_UNROLL + r
            hf = idx // n_chunk
            off_re = pl.multiple_of(hf * (2 * S5_HALF_STATE) + (idx % n_chunk) * LANE, LANE)
            scan_chunk(off_re, pl.multiple_of(off_re + S5_HALF_STATE, LANE))
        return 0

    lax.fori_loop(0, 2 * n_chunk // S5_UNROLL, chunk_group, 0)

    if need_out:
        for hf in range(2):
            y_half = _dot(h_sc[:, hf * 2 * S5_HALF_STATE:(hf + 1) * 2 * S5_HALF_STATE], c_ref[hf])
            for k in range(half_u // LANE):
                y_sc[hf * (half_u // LANE) + k] = y_half[:, k * LANE:(k + 1) * LANE]
        for k in range(n_u):
            for s in range(S5_SEG):
                y_ref[s * S5_STEPS:(s + 1) * S5_STEPS, k * LANE:(k + 1) * LANE] = (
                    y_sc[k, pl.ds(s, S5_STEPS, stride=S5_SEG), :])

    @pl.when(blk == pl.num_programs(1) - 1)
    def _():
        hf_ref[...] = carry_sc[...]


def _s5_scan(small, b_blk, c_blk, tabs, h0, *, batch, direction, need_out):
    n = small.shape[0]
    length = n // batch
    tb = S5_BLOCK
    n_blocks = length // tb
    reverse = direction == 1
    if reverse:
        row = lambda b, i: b * n_blocks + (n_blocks - 1 - i)
    else:
        row = lambda b, i: b * n_blocks + i
    const2 = lambda shape: pl.BlockSpec(shape, lambda b, i: (0,) * len(shape))
    n_u = S5_WIDTH // LANE
    u_spec = lambda k: pl.BlockSpec((tb, LANE), lambda b, i: (row(b, i), BD_PAD // LANE + k))
    in_specs = [u_spec(k) for k in range(n_u)] + [const2(b_blk.shape)]
    args = [small] * n_u + [b_blk]
    if need_out:
        in_specs.append(const2(c_blk.shape))
        args.append(c_blk)
    in_specs += [const2(t.shape) for t in tabs]
    in_specs.append(pl.BlockSpec((None, 8, S5_LANES), lambda b, i: (b, 0, 0)))
    args += list(tabs) + [h0]
    carry_spec = pl.BlockSpec((None, 8, S5_LANES), lambda b, i: (b, 0, 0))
    carry_shape = jax.ShapeDtypeStruct((batch, 8, S5_LANES), F32)
    scratch = [pltpu.VMEM((tb, S5_WIDTH), F32), pltpu.VMEM((tb, S5_LANES), F32)]
    if need_out:
        out_specs = [pl.BlockSpec((tb, S5_WIDTH), lambda b, i: (row(b, i), 0)), carry_spec]
        out_shape = [jax.ShapeDtypeStruct((n, S5_WIDTH), F32), carry_shape]
        scratch += [pltpu.VMEM((tb, S5_LANES), BF16), pltpu.VMEM((n_u, tb, LANE), F32)]
    else:
        out_specs = [carry_spec]
        out_shape = [carry_shape]
    scratch.append(pltpu.VMEM((8, S5_LANES), F32))
    res = pl.pallas_call(
        functools.partial(_s5_kernel, reverse=reverse, need_out=need_out),
        grid=(batch, n_blocks),
        in_specs=in_specs,
        out_specs=out_specs,
        out_shape=out_shape,
        scratch_shapes=scratch,
        compiler_params=pltpu.CompilerParams(dimension_semantics=("arbitrary", "arbitrary"),
                                             vmem_limit_bytes=VMEM_LIMIT),
        name=f"s5_d{direction}" + ("_out" if need_out else "_state"),
    )(*args)
    return res if need_out else (None, res[0])


def _out_kernel(x_ref, gate_ref, of_ref, ob_ref, dnw_ref, szdn_ref, sgdn_ref, sgs5_ref, szs5_ref,
                yf_ref, yb_ref, small_ref, d_ref, gluw_ref, glub_ref, wpd_ref, wps_ref, wo_ref, fnw_ref,
                o_ref):
    o = of_ref[...] + ob_ref[...]
    parts = []
    for h in range(DN_HEADS):
        t = o[:, h * DN_DV:(h + 1) * DN_DV]
        parts.append(t * lax.rsqrt(jnp.mean(t * t, axis=-1, keepdims=True) + NORM_EPS))
    y_dn = jnp.concatenate(parts, axis=1) * dnw_ref[...] * szdn_ref[...]
    u = small_ref[:, BD_PAD:BD_PAD + S5_WIDTH]
    y = d_ref[...] * u + yf_ref[...] + yb_ref[...]
    y = 0.5 * y * (1.0 + jnp.tanh(0.7978845608028654 * (y + 0.044715 * (y * y * y))))
    y = y * _sigmoid(_dot(y.astype(BF16), gluw_ref[...]) + glub_ref[...])
    y_s5 = y * szs5_ref[...]
    merged = (sgdn_ref[...] * _dot(y_dn.astype(BF16), wpd_ref[...])
              + sgs5_ref[...] * _dot(y_s5.astype(BF16), wps_ref[...]))
    out = _dot(merged.astype(BF16), wo_ref[...])
    xo = x_ref[...] + gate_ref[...] * out
    o_ref[...] = xo * lax.rsqrt(jnp.mean(xo * xo, axis=-1, keepdims=True) + NORM_EPS) * fnw_ref[...]


def _output_stage(x2, mod3, o_f, o_b, dn_norm_w, gates, y_f, y_b, small, s5_d, glu_w, glu_b,
                  w_proj_dn, w_proj_s5, w_out, final_norm_w, *, batch, tm):
    n = x2.shape[0]
    tiles_per_batch = n // batch // tm
    tok = lambda width, col: pl.BlockSpec((tm, width), lambda i: (i, col))
    const = lambda shape: pl.BlockSpec(shape, lambda i: (0, 0))
    in_specs = [
        tok(D_MODEL, 0),
        pl.BlockSpec((None, 1, D_MODEL), lambda i: (i // tiles_per_batch, 0, 2)),
        tok(DN_WIDTH, 0), tok(DN_WIDTH, 0), const((1, DN_WIDTH)),
        tok(D_MODEL, 0), tok(D_MODEL, 1), tok(D_MODEL, 2), tok(S5_WIDTH, 3 * D_MODEL // S5_WIDTH),
        tok(S5_WIDTH, 0), tok(S5_WIDTH, 0), tok(SMALL_W, 0), const((1, S5_WIDTH)),
        const((S5_WIDTH, S5_WIDTH)), const((1, S5_WIDTH)),
        const((DN_WIDTH, D_MODEL)), const((S5_WIDTH, D_MODEL)), const((D_MODEL, D_MODEL)),
        const((1, D_MODEL)),
    ]
    return pl.pallas_call(
        _out_kernel,
        grid=(n // tm,),
        in_specs=in_specs,
        out_specs=pl.BlockSpec((tm, D_MODEL), lambda i: (i, 0)),
        out_shape=jax.ShapeDtypeStruct((n, D_MODEL), F32),
        compiler_params=pltpu.CompilerParams(dimension_semantics=("arbitrary",),
                                             vmem_limit_bytes=VMEM_LIMIT),
        name="output_stage",
    )(x2, mod3, o_f, o_b, dn_norm_w, gates, gates, gates, gates, y_f, y_b, small, s5_d,
      glu_w, glu_b, w_proj_dn, w_proj_s5, w_out, final_norm_w)


def _gate_lane_params(a_log, dt_bias):
    par = jnp.zeros((8, LANE), F32)
    par = par.at[0, 2 * DN_HEADS:4 * DN_HEADS].set(a_log.reshape(-1).astype(F32))
    par = par.at[1, 2 * DN_HEADS:4 * DN_HEADS].set(dt_bias.reshape(-1).astype(F32))
    return par


def _expand_matrix(first_lane):
    lane = jnp.arange(LANE)[:, None]
    head = jnp.arange(DN_WIDTH)[None, :] // DN_DV
    return (lane == first_lane + head).astype(BF16)


def kernel(x, c, ctx, c_ctx, w_ada, b_ada, norm_w, w_in, conv_w, dn_A_log, dn_dt_bias, dn_norm_w,
           s5_lam_re, s5_lam_im, s5_log_step, s5_B_re, s5_B_im, s5_C_re, s5_C_im, s5_D, glu_w, glu_b,
           w_proj_dn, w_proj_s5, w_out, final_norm_w):
    batch, seq, d_model = x.shape
    ctx_len = ctx.shape[1]
    depth = w_in.shape[0]
    assert d_model == D_MODEL and depth == 1 and batch <= 4
    assert seq % 512 == 0 and ctx_len % S5_BLOCK == 0 and ctx_len & (ctx_len - 1) == 0
    i = 0
    x2 = x.reshape(batch * seq, D_MODEL)
    ctx2 = ctx.reshape(batch * ctx_len, D_MODEL)

    w_full = w_in[i]
    w_qkv = w_full[:, :QKV_W].astype(BF16)
    w_small = jnp.concatenate(
        [w_full[:, O_BETA:O_U], jnp.zeros((D_MODEL, BD_PAD - (O_U - O_BETA)), w_full.dtype),
         w_full[:, O_U:STATE_COLS]], axis=1).astype(BF16)
    w_gate = jnp.concatenate(
        [w_full[:, O_ZDN:O_ZS5], w_full[:, O_GDN:O_GS5], w_full[:, O_GS5:IN_WIDTH],
         w_full[:, O_ZS5:O_GDN]], axis=1).astype(BF16)
    conv_w8 = jnp.concatenate([conv_w[i], jnp.zeros((8 - CONV_K, QKV_W), conv_w.dtype)], axis=0).astype(F32)
    nw = norm_w[i].reshape(1, D_MODEL).astype(F32)
    e_beta = [_expand_matrix(DN_HEADS * d) for d in range(2)]
    e_gate = [_expand_matrix(2 * DN_HEADS + DN_HEADS * d) for d in range(2)]
    par = _gate_lane_params(dn_A_log[i], dn_dt_bias[i])

    cvec = jnp.zeros((8, D_MODEL), F32).at[:batch].set(c.astype(F32)).at[batch].set(c_ctx.astype(F32))
    mod = _mod_vectors(cvec, w_ada[i].astype(F32), b_ada[i].reshape(1, -1).astype(F32))
    mod3 = mod.reshape(8, 1, 3 * D_MODEL)

    pow_re, pow_im, bbar_re, bbar_im = _s5_prepare(
        s5_lam_re[i], s5_lam_im[i], s5_log_step[i], s5_B_re[i], s5_B_im[i])
    b_blk, c_blk = _s5_block_matrices(
        bbar_re, bbar_im,
        s5_C_re[i].astype(F32), s5_C_im[i].astype(F32))
    pad8 = lambda t: jnp.concatenate([t, jnp.zeros((-t.shape[0] % 8,) + t.shape[1:], t.dtype)], axis=0)
    s5_tabs = [tuple(pad8(_s5_lane_layout(t, t)[:, d]) for t in (pow_re, pow_im)) for d in range(2)]

    ctx_row = lambda t: batch
    ctx_qkv = _inproj(ctx2, nw, mod3, ctx_row, w_qkv, conv_w8, epilogue="qkv", rows=ctx_len, tm=ctx_len)
    ctx_small = _inproj(ctx2, nw, mod3, ctx_row, w_small, None, epilogue="raw", rows=ctx_len, tm=ctx_len)
    zero_state = jnp.zeros((2, batch, DN_HEADS, DN_DK, DN_DV), F32)
    zero_h = jnp.zeros((batch, 8, S5_LANES), F32)
    _, _, s_ctx = _delta_rule(ctx_qkv, ctx_small, par, e_beta, e_gate, zero_state, batch=batch, need_out=False)
    h_ctx = []
    for d in range(2):
        _, h_fin = _s5_scan(ctx_small, b_blk[d], None, s5_tabs[d], zero_h,
                            batch=batch, direction=d, need_out=False)
        h_ctx.append(h_fin)

    tm = 512
    tiles_per_batch = seq // tm
    lat_row = lambda t: t // tiles_per_batch
    qkv = _inproj(x2, nw, mod3, lat_row, w_qkv, conv_w8, epilogue="qkv", rows=GRID_W, tm=tm)
    small = _inproj(x2, nw, mod3, lat_row, w_small, None, epilogue="raw", rows=GRID_W, tm=tm)
    gates = _inproj(x2, nw, mod3, lat_row, w_gate, None, epilogue="gate", rows=GRID_W, tm=tm)
    o_f, o_b, _ = _delta_rule(qkv, small, par, e_beta, e_gate, s_ctx, batch=batch, need_out=True)
    o_dir = [o_f, o_b]
    y_dir = []
    for d in range(2):
        y_d, _ = _s5_scan(small, b_blk[d], c_blk[d], s5_tabs[d], h_ctx[d],
                          batch=batch, direction=d, need_out=True)
        y_dir.append(y_d)

    out = _output_stage(
        x2, mod3, o_dir[0], o_dir[1], jnp.tile(dn_norm_w[i].astype(F32), DN_HEADS).reshape(1, DN_WIDTH),
        gates, y_dir[0], y_dir[1], small, s5_D[i].reshape(1, S5_WIDTH).astype(F32),
        glu_w[i].astype(BF16), glu_b[i].reshape(1, S5_WIDTH).astype(F32),
        w_proj_dn[i].astype(BF16), w_proj_s5[i].astype(BF16), w_out[i].astype(BF16),
        final_norm_w.reshape(1, D_MODEL).astype(F32), batch=batch, tm=256)
    return out.reshape(batch, seq, D_MODEL)
```

```python
import functools

import jax
import jax.numpy as jnp
from jax import lax
from jax.experimental import pallas as pl
from jax.experimental.pallas import tpu as pltpu

F32 = jnp.float32
BF16 = jnp.bfloat16
HIGHEST = lax.Precision.HIGHEST

D_MODEL = 1024
NORM_EPS = 1e-6
GRID_W = 64
DN_HEADS = 8
DN_DK = 128
DN_DV = 128
DN_WIDTH = DN_HEADS * DN_DV
CONV_K = 5
CHUNK = 64
S5_WIDTH = 512
S5_GROUP = 16
S5_GROUPS = S5_WIDTH // S5_GROUP
S5_STATE = 64
S5_HALF_GROUPS = S5_GROUPS // 2
S5_HALF_STATE = S5_HALF_GROUPS * S5_STATE
S5_LANES = 4 * S5_HALF_STATE
S5_BLOCK = 256
S5_SEG = 8
S5_STEPS = S5_BLOCK // S5_SEG
S5_SEG_LOG = 3
QK_W = DN_HEADS * DN_DK
QKV_W = 2 * QK_W + DN_WIDTH
O_BETA = QKV_W
O_DECAY = O_BETA + 2 * DN_HEADS
O_U = O_DECAY + 2 * DN_HEADS
STATE_COLS = O_U + S5_WIDTH
O_ZDN = STATE_COLS
O_ZS5 = O_ZDN + DN_WIDTH
O_GDN = O_ZS5 + S5_WIDTH
O_GS5 = O_GDN + D_MODEL
IN_WIDTH = O_GS5 + D_MODEL

LANE = 128
BD_PAD = LANE
SMALL_W = BD_PAD + S5_WIDTH
GATE_W = 3 * D_MODEL + S5_WIDTH
COL_TILE = 512
VMEM_LIMIT = 48 * 1024 * 1024


def _sigmoid(x):
    return 0.5 * jnp.tanh(0.5 * x) + 0.5


def _silu(x):
    half = 0.5 * x
    return half * jnp.tanh(half) + half


def _softplus(x):
    return jnp.maximum(x, 0.0) + jnp.log1p(jnp.exp(-jnp.abs(x)))


def _dot(a, b):
    return jnp.dot(a, b, preferred_element_type=F32)


def _dot_hi(a, b):
    return jnp.dot(a, b, precision=HIGHEST, preferred_element_type=F32)


def _dot_nt(a, b):
    return lax.dot_general(a, b, (((1,), (1,)), ((), ())), preferred_element_type=F32)


def _dot_tn(a, b):
    return lax.dot_general(a, b, (((0,), (0,)), ((), ())), preferred_element_type=F32)


def _split3(x):
    hi = x.astype(BF16)
    r1 = x - hi.astype(F32)
    mid = r1.astype(BF16)
    lo = (r1 - mid.astype(F32)).astype(BF16)
    return hi, mid, lo


def _dot_sel_rhs(x, sel):
    hi, mid, lo = _split3(x)
    return _dot(hi, sel) + _dot(mid, sel) + _dot(lo, sel)


def _dot_sel_lhs(sel, x):
    hi, mid, lo = _split3(x)
    return _dot(sel, hi) + _dot(sel, mid) + _dot(sel, lo)


def _mod_kernel(c_ref, w_ref, b_ref, o_ref):
    c = c_ref[...]
    s = _silu(c)
    o_ref[...] = _dot_hi(s, w_ref[...]) + b_ref[...]


def _mod_vectors(cvec, w_ada, b_ada):
    n = w_ada.shape[1]
    return pl.pallas_call(
        _mod_kernel,
        grid=(n // COL_TILE,),
        in_specs=[
            pl.BlockSpec((8, D_MODEL), lambda j: (0, 0)),
            pl.BlockSpec((D_MODEL, COL_TILE), lambda j: (0, j)),
            pl.BlockSpec((1, COL_TILE), lambda j: (0, j)),
        ],
        out_specs=pl.BlockSpec((8, COL_TILE), lambda j: (0, j)),
        out_shape=jax.ShapeDtypeStruct((8, n), F32),
        compiler_params=pltpu.CompilerParams(dimension_semantics=("arbitrary",)),
        name="mod_vectors",
    )(cvec, w_ada, b_ada)


def _s5_prep_kernel(lre_ref, lim_ref, ls_ref, bre_ref, bim_ref, pr_ref, pi_ref, bbr_ref, bbi_ref):
    lam_re = lre_ref[...]
    lam_im = lim_ref[...]
    dt = jnp.exp(ls_ref[...])
    mag = jnp.exp(lam_re * dt)
    abar_re = mag * jnp.cos(lam_im * dt)
    abar_im = mag * jnp.sin(lam_im * dt)
    num_re = abar_re - 1.0
    num_im = abar_im
    den = lam_re * lam_re + lam_im * lam_im
    f_re = (num_re * lam_re + num_im * lam_im) / den
    f_im = (num_im * lam_re - num_re * lam_im) / den
    b_re = bre_ref[...]
    b_im = bim_ref[...]
    bbr_ref[...] = f_re * b_re - f_im * b_im
    bbi_ref[...] = f_re * b_im + f_im * b_re
    pr_ref[0] = abar_re
    pi_ref[0] = abar_im
    pr, pi = abar_re, abar_im
    n = 1
    while n < S5_STEPS:
        pr, pi = pr * pr - pi * pi, 2.0 * pr * pi
        n *= 2
    for k in range(S5_SEG_LOG):
        pr_ref[1 + k] = pr
        pi_ref[1 + k] = pi
        pr, pi = pr * pr - pi * pi, 2.0 * pr * pi


def _s5_prepare(lam_re, lam_im, log_step, b_re, b_im):
    dg = 2 * S5_GROUPS
    lre = lam_re.reshape(dg, 1, S5_STATE)
    lim = lam_im.reshape(dg, 1, S5_STATE)
    ls = jnp.broadcast_to(log_step.reshape(dg, 1, 1), (dg, 1, S5_STATE))
    bre = jnp.swapaxes(b_re.reshape(dg, S5_STATE, S5_GROUP), 1, 2)
    bim = jnp.swapaxes(b_im.reshape(dg, S5_STATE, S5_GROUP), 1, 2)
    full3 = lambda shape: pl.BlockSpec(shape, lambda: (0,) * len(shape))
    pow_shape = (1 + S5_SEG_LOG, dg, 1, S5_STATE)
    pr, pi, bbr, bbi = pl.pallas_call(
        _s5_prep_kernel,
        in_specs=[full3(lre.shape), full3(lim.shape), full3(ls.shape), full3(bre.shape), full3(bim.shape)],
        out_specs=[full3(pow_shape), full3(pow_shape), full3(bre.shape), full3(bim.shape)],
        out_shape=[jax.ShapeDtypeStruct(pow_shape, F32), jax.ShapeDtypeStruct(pow_shape, F32),
                   jax.ShapeDtypeStruct(bre.shape, F32), jax.ShapeDtypeStruct(bim.shape, F32)],
        name="s5_discretise",
    )(lre.astype(F32), lim.astype(F32), ls.astype(F32), bre.astype(F32), bim.astype(F32))
    per_dir = lambda t: t.reshape(t.shape[0], 2, S5_GROUPS, S5_STATE)
    bbar_re = bbr.reshape(2, S5_GROUPS, S5_GROUP, S5_STATE)
    bbar_im = bbi.reshape(2, S5_GROUPS, S5_GROUP, S5_STATE)
    return per_dir(pr), per_dir(pi), bbar_re, bbar_im


def _s5_lane_layout(re, im):
    lead = re.shape[:-2]
    re = re.reshape(*lead, 2, S5_HALF_STATE)
    im = im.reshape(*lead, 2, S5_HALF_STATE)
    return jnp.concatenate([re, im], axis=-1).reshape(*lead, S5_LANES)


def _s5_block_matrices(bbar_re, bbar_im, c_re, c_im):
    eye = jnp.eye(S5_HALF_GROUPS, dtype=F32)

    def in_map(bb):
        bb = bb.reshape(2, 2, S5_HALF_GROUPS, S5_GROUP, S5_STATE)
        m = jnp.einsum('dhgcp,gk->dhgckp', bb, eye)
        return m.reshape(2, 2, S5_HALF_GROUPS * S5_GROUP, S5_HALF_STATE)

    def out_map(cc):
        cc = cc.reshape(2, 2, S5_HALF_GROUPS, S5_GROUP, S5_STATE)
        m = jnp.einsum('dhgcp,gk->dhgpkc', cc, eye)
        return m.reshape(2, 2, S5_HALF_STATE, S5_HALF_GROUPS * S5_GROUP)

    b_blk = jnp.concatenate([in_map(bbar_re), in_map(bbar_im)], axis=-1)
    c_blk = jnp.concatenate([out_map(c_re), out_map(-c_im)], axis=-2)
    return b_blk.astype(BF16), c_blk.astype(BF16)


def _conv_silu(acc, cw, rows):
    tm = acc.shape[0]
    pos = lax.broadcasted_iota(jnp.int32, acc.shape, 0) & (rows - 1)
    half = CONV_K // 2
    y = acc * cw[half:half + 1, :]
    for j in range(CONV_K):
        s = j - half
        if s == 0:
            continue
        shifted = pltpu.roll(acc, (-s) % tm, axis=0)
        valid = (pos >= -s) if s < 0 else (pos < rows - s)
        y = y + jnp.where(valid, shifted, 0.0) * cw[j:j + 1, :]
    return _silu(y)


def _l2norm_heads(y, scale):
    outs = []
    for h in range(y.shape[1] // DN_DK):
        t = y[:, h * DN_DK:(h + 1) * DN_DK]
        outs.append(t * (lax.rsqrt(jnp.sum(t * t, axis=-1, keepdims=True) + NORM_EPS) * scale))
    return jnp.concatenate(outs, axis=1)


def _inproj_kernel(*refs, epilogue, rows):
    if epilogue == "qkv":
        x_ref, nw_ref, sc_ref, sh_ref, w_ref, cw_ref, o_ref, h_sc = refs
    else:
        x_ref, nw_ref, sc_ref, sh_ref, w_ref, o_ref, h_sc = refs
    j = pl.program_id(1)

    @pl.when(j == 0)
    def _():
        x = x_ref[...]
        y = x * lax.rsqrt(jnp.mean(x * x, axis=-1, keepdims=True) + NORM_EPS) * nw_ref[...]
        h_sc[...] = (y * (1.0 + sc_ref[...]) + sh_ref[...]).astype(BF16)

    acc = _dot(h_sc[...], w_ref[...])
    if epilogue == "raw":
        o_ref[...] = acc
    elif epilogue == "gate":
        is_silu = jnp.logical_or(j < 2, j == 6)

        @pl.when(is_silu)
        def _():
            o_ref[...] = _silu(acc).astype(o_ref.dtype)

        @pl.when(jnp.logical_not(is_silu))
        def _():
            o_ref[...] = _sigmoid(acc).astype(o_ref.dtype)
    else:
        y = _conv_silu(acc, cw_ref[...], rows)
        n_qk_tiles = QK_W // COL_TILE

        @pl.when(j < 2 * n_qk_tiles)
        def _():
            scale = jnp.where(j < n_qk_tiles, DN_DK ** -0.5, 1.0)
            o_ref[...] = _l2norm_heads(y, scale)

        @pl.when(j >= 2 * n_qk_tiles)
        def _():
            o_ref[...] = y


def _inproj(x2, norm_w, mod3, mod_row, w, conv_w, *, epilogue, rows, tm):
    n = x2.shape[0]
    width = w.shape[1]
    tn = COL_TILE if width % COL_TILE == 0 else width
    grid = (n // tm, width // tn)
    w_tiles = jnp.swapaxes(w.reshape(D_MODEL, width // tn, tn), 0, 1)
    vec = lambda k: pl.BlockSpec((None, 1, D_MODEL), lambda i, j: (mod_row(i), 0, k))
    in_specs = [
        pl.BlockSpec((tm, D_MODEL), lambda i, j: (i, 0)),
        pl.BlockSpec((1, D_MODEL), lambda i, j: (0, 0)),
        vec(1),
        vec(0),
        pl.BlockSpec((None, D_MODEL, tn), lambda i, j: (j, 0, 0)),
    ]
    args = [x2, norm_w, mod3, mod3, w_tiles]
    if epilogue == "qkv":
        in_specs.append(pl.BlockSpec((8, tn), lambda i, j: (0, j)))
        args.append(conv_w)
    return pl.pallas_call(
        functools.partial(_inproj_kernel, epilogue=epilogue, rows=rows),
        grid=grid,
        in_specs=in_specs,
        out_specs=pl.BlockSpec((tm, tn), lambda i, j: (i, j)),
        out_shape=jax.ShapeDtypeStruct((n, width), BF16 if epilogue == "gate" else F32),
        scratch_shapes=[pltpu.VMEM((tm, D_MODEL), BF16)],
        compiler_params=pltpu.CompilerParams(dimension_semantics=("arbitrary", "arbitrary"),
                                             vmem_limit_bytes=VMEM_LIMIT),
        name="inproj_" + epilogue,
    )(*args)


def _block_masks(n, reverse):
    ri = lax.broadcasted_iota(jnp.int32, (n, n), 0)
    ci = lax.broadcasted_iota(jnp.int32, (n, n), 1)
    lo, hi = (ri, ci) if reverse else (ci, ri)
    masks = []
    b = 1
    while b < n:
        masks.append(((hi ^ lo) < 2 * b) & ((hi & b) != 0) & ((lo & b) == 0))
        b *= 2
    return masks


def _delta_chunk_prep(q, k, v, bd, par, e_beta, e_gate, direction):
    reverse = direction == 1
    beta_all = _sigmoid(bd)
    g_all = -jnp.exp(par[0:1, :]) * _softplus(bd + par[1:2, :])
    ri = lax.broadcasted_iota(jnp.int32, (CHUNK, CHUNK), 0)
    ci = lax.broadcasted_iota(jnp.int32, (CHUNK, CHUNK), 1)
    incl = (ri <= ci) if reverse else (ri >= ci)
    strict = (ri < ci) if reverse else (ri > ci)
    tri = jnp.where(incl, 1.0, 0.0).astype(BF16)

    g_cum = _dot_sel_lhs(tri, g_all)
    g_cum_t = jnp.transpose(jnp.concatenate([g_cum, jnp.zeros_like(g_cum)], axis=0))
    beta_x = _dot(beta_all.astype(BF16), e_beta)
    g_hi = g_cum.astype(BF16)
    g_x = _dot(g_hi, e_gate) + _dot((g_cum - g_hi.astype(F32)).astype(BF16), e_gate)
    edge = 0 if reverse else CHUNK - 1
    g_last = g_x[edge:edge + 1, :]
    e_g = jnp.exp(g_x)
    a_end = jnp.exp(g_last)
    kb = k * beta_x
    k_bf = k.astype(BF16)
    kq = jnp.concatenate([kb, q], axis=0).astype(BF16)
    vb = (v * beta_x).astype(BF16)
    kbe = (kb * e_g).astype(BF16)
    qd = (q * e_g).astype(BF16)
    kend = (k * jnp.exp(g_last - g_x)).astype(BF16)

    heads = []
    for h in range(DN_HEADS):
        sl = slice(h * DN_DK, (h + 1) * DN_DK)
        gate_lane = 2 * DN_HEADS + DN_HEADS * direction + h
        diff = g_x[:, h * DN_DK:h * DN_DK + CHUNK] - g_cum_t[gate_lane:gate_lane + 1, 0:CHUNK]
        heads.append(dict(
            dec=jnp.exp(jnp.where(incl, diff, -jnp.inf)),
            kq=kq[:, sl], k=k_bf[:, sl],
            rhs=jnp.concatenate([vb[:, sl], kbe[:, sl]], axis=1),
            qd=qd[:, sl], kend=kend[:, sl], a_end=a_end[:, sl]))
    return heads, strict, _block_masks(CHUNK, reverse)


def _delta_kernel(*refs, need_out):
    ins, rest = refs[:14], refs[14:]
    (qf_ref, kf_ref, vf_ref, bdf_ref, qb_ref, kb_ref, vb_ref, bdb_ref,
     par_ref, eb0_ref, eg0_ref, eb1_ref, eg1_ref, s0_ref) = ins
    if need_out:
        of_ref, ob_ref, sf_ref, s_sc = rest
    else:
        sf_ref, s_sc = rest
    c = pl.program_id(0)
    batch = qf_ref.shape[0]

    @pl.when(c == 0)
    def _():
        s_sc[...] = s0_ref[...]

    par = par_ref[...]
    units = []
    for b in range(batch):
        for d, (q_ref, k_ref, v_ref, bd_ref, eb_ref, eg_ref) in enumerate(
                ((qf_ref, kf_ref, vf_ref, bdf_ref, eb0_ref, eg0_ref),
                 (qb_ref, kb_ref, vb_ref, bdb_ref, eb1_ref, eg1_ref))):
            heads, strict, masks = _delta_chunk_prep(q_ref[b], k_ref[b], v_ref[b], bd_ref[b], par,
                                                     eb_ref[...], eg_ref[...], d)
            for h, unit in enumerate(heads):
                unit.update(b=b, d=d, h=h, strict=strict, masks=masks)
                units.append(unit)

    ri = lax.broadcasted_iota(jnp.int32, (CHUNK, CHUNK), 0)
    ci = lax.broadcasted_iota(jnp.int32, (CHUNK, CHUNK), 1)
    eye = jnp.where(ri == ci, 1.0, 0.0).astype(F32)
    for u in units:
        aq = _dot_nt(u["kq"], u["k"])
        u["am"] = jnp.where(u["strict"], aq[:CHUNK] * u["dec"], 0.0)
        u["attn"] = (aq[CHUNK:] * u["dec"]).astype(BF16)
        u["inv"] = eye - jnp.where(u["masks"][0], u["am"], 0.0)
    for level in range(1, len(units[0]["masks"])):
        for u in units:
            e = jnp.where(u["masks"][level], u["am"], 0.0).astype(BF16)
            u["inv_bf"] = u["inv"].astype(BF16)
            u["t"] = _dot(e, u["inv_bf"]).astype(BF16)
        for u in units:
            u["inv"] = u["inv"] - _dot(u["inv_bf"], u["t"])
    for u in units:
        u["uw"] = _dot(u["inv"].astype(BF16), u["rhs"])
    for u in units:
        u["s"] = s_sc[u["d"], u["b"], u["h"]]
        lhs = jnp.concatenate([u["uw"][:, DN_DV:].astype(BF16), u["qd"]], axis=0)
        u["ws"] = _dot(lhs, u["s"].astype(BF16))
    outs = {}
    for u in units:
        v_new = (u["uw"][:, :DN_DV] - u["ws"][:CHUNK]).astype(BF16)
        if need_out:
            outs.setdefault((u["d"], u["b"]), []).append(u["ws"][CHUNK:] + _dot(u["attn"], v_new))
        s_sc[u["d"], u["b"], u["h"]] = u["s"] * u["a_end"] + _dot_tn(u["kend"], v_new)

    if need_out:
        for b in range(batch):
            of_ref[b] = jnp.concatenate(outs[(0, b)], axis=1)
            ob_ref[b] = jnp.concatenate(outs[(1, b)], axis=1)

    @pl.when(c == pl.num_programs(0) - 1)
    def _():
        sf_ref[...] = s_sc[...]


def _delta_rule(qkv, small, par, e_beta, e_gate, s0, *, batch, need_out):
    n = qkv.shape[0]
    length = n // batch
    n_chunks = length // CHUNK
    qkv3 = qkv.reshape(batch, length, QKV_W)
    small3 = small.reshape(batch, length, SMALL_W)
    fwd = lambda c: c
    bwd = lambda c: n_chunks - 1 - c
    in_specs = []
    for row in (fwd, bwd):
        in_specs += [
            pl.BlockSpec((batch, CHUNK, QK_W), lambda c, row=row: (0, row(c), 0)),
            pl.BlockSpec((batch, CHUNK, QK_W), lambda c, row=row: (0, row(c), 1)),
            pl.BlockSpec((batch, CHUNK, DN_WIDTH), lambda c, row=row: (0, row(c), 2)),
            pl.BlockSpec((batch, CHUNK, BD_PAD), lambda c, row=row: (0, row(c), 0)),
        ]
    const = lambda shape: pl.BlockSpec(shape, lambda c: (0,) * len(shape))
    state_dims = (2, batch, DN_HEADS, DN_DK, DN_DV)
    in_specs += [const((8, LANE)), const((LANE, DN_WIDTH)), const((LANE, DN_WIDTH)),
                 const((LANE, DN_WIDTH)), const((LANE, DN_WIDTH)), const(state_dims)]
    state_shape = jax.ShapeDtypeStruct(state_dims, F32)
    if need_out:
        out_specs = [pl.BlockSpec((batch, CHUNK, DN_WIDTH), lambda c: (0, fwd(c), 0)),
                     pl.BlockSpec((batch, CHUNK, DN_WIDTH), lambda c: (0, bwd(c), 0)), const(state_dims)]
        out_shape = [jax.ShapeDtypeStruct((batch, length, DN_WIDTH), F32)] * 2 + [state_shape]
    else:
        out_specs = [const(state_dims)]
        out_shape = [state_shape]
    res = pl.pallas_call(
        functools.partial(_delta_kernel, need_out=need_out),
        grid=(n_chunks,),
        in_specs=in_specs,
        out_specs=out_specs,
        out_shape=out_shape,
        scratch_shapes=[pltpu.VMEM(state_dims, F32)],
        compiler_params=pltpu.CompilerParams(dimension_semantics=("arbitrary",),
                                             vmem_limit_bytes=VMEM_LIMIT),
        name="delta_out" if need_out else "delta_state",
    )(qkv3, qkv3, qkv3, small3, qkv3, qkv3, qkv3, small3, par,
      e_beta[0], e_gate[0], e_beta[1], e_gate[1], s0)
    if not need_out:
        return None, None, res[0]
    return res[0].reshape(n, DN_WIDTH), res[1].reshape(n, DN_WIDTH), res[2]


def _cmul_add(a_re, a_im, h_re, h_im, x_re, x_im):
    return a_re * h_re - a_im * h_im + x_re, a_re * h_im + a_im * h_re + x_im


def _s5_kernel(*refs, reverse, need_out):
    n_u = S5_WIDTH // LANE
    u_refs, refs = refs[:n_u], refs[n_u:]
    if need_out:
        b_ref, c_ref, pr_ref, pi_ref, h0_ref, y_ref, hf_ref, up_sc, x_sc, h_sc, y_sc, carry_sc = refs
    else:
        b_ref, pr_ref, pi_ref, h0_ref, hf_ref, up_sc, x_sc, carry_sc = refs
    blk = pl.program_id(1)

    @pl.when(blk == 0)
    def _():
        carry_sc[...] = h0_ref[...]

    for j in range(S5_STEPS):
        for k in range(n_u):
            up_sc[j * S5_SEG:(j + 1) * S5_SEG, k * LANE:(k + 1) * LANE] = (
                u_refs[k][pl.ds(j, S5_SEG, stride=S5_STEPS), :])
    u = up_sc[...].astype(BF16)
    half_u = S5_WIDTH // 2
    for hf in range(2):
        x_sc[:, hf * 2 * S5_HALF_STATE:(hf + 1) * 2 * S5_HALF_STATE] = _dot(
            u[:, hf * half_u:(hf + 1) * half_u], b_ref[hf])

    order = list(range(S5_STEPS - 1, -1, -1)) if reverse else list(range(S5_STEPS))
    seg = lax.broadcasted_iota(jnp.int32, (S5_SEG, LANE), 0)
    first_seg = S5_SEG - 1 if reverse else 0
    last_seg = 0 if reverse else S5_SEG - 1
    n_chunk = S5_HALF_STATE // LANE

    def scan_chunk(off_re, off_im):
        re_l, im_l = pl.ds(off_re, LANE), pl.ds(off_im, LANE)
        a_re, a_im = pr_ref[0:1, re_l], pi_ref[0:1, re_l]
        h_re = jnp.zeros((S5_SEG, LANE), F32)
        h_im = jnp.zeros((S5_SEG, LANE), F32)
        for j in order:
            rows = pl.ds(j * S5_SEG, S5_SEG)
            h_re, h_im = _cmul_add(a_re, a_im, h_re, h_im, x_sc[rows, re_l], x_sc[rows, im_l])
        toward = 1 if not reverse else S5_SEG - 1
        f_re = jnp.where(seg == first_seg, carry_sc[0:1, re_l], pltpu.roll(h_re, toward, axis=0))
        f_im = jnp.where(seg == first_seg, carry_sc[0:1, im_l], pltpu.roll(h_im, toward, axis=0))
        for k in range(S5_SEG_LOG):
            d = 1 << k
            valid = (seg < S5_SEG - d) if reverse else (seg >= d)
            amount = S5_SEG - d if reverse else d
            p_re = jnp.where(valid, pltpu.roll(f_re, amount, axis=0), 0.0)
            p_im = jnp.where(valid, pltpu.roll(f_im, amount, axis=0), 0.0)
            f_re, f_im = _cmul_add(pr_ref[1 + k:2 + k, re_l], pi_ref[1 + k:2 + k, re_l], p_re, p_im, f_re, f_im)
        n_re, n_im = _cmul_add(pr_ref[1:2, re_l], pi_ref[1:2, re_l], f_re, f_im, h_re, h_im)
        carry_sc[0:1, re_l] = n_re[last_seg:last_seg + 1, :]
        carry_sc[0:1, im_l] = n_im[last_seg:last_seg + 1, :]
        if need_out:
            h_re, h_im = f_re, f_im
            for j0 in range(0, S5_STEPS, 2):
                pair = {}
                for j in order[j0:j0 + 2]:
                    rows = pl.ds(j * S5_SEG, S5_SEG)
                    h_re, h_im = _cmul_add(a_re, a_im, h_re, h_im, x_sc[rows, re_l], x_sc[rows, im_l])
                    pair[j] = (h_re, h_im)
                lo = min(pair)
                rows2 = pl.ds(lo * S5_SEG, 2 * S5_SEG)
                h_sc[rows2, re_l] = jnp.concatenate([pair[lo][0], pair[lo + 1][0]], axis=0).astype(BF16)
                h_sc[rows2, im_l] = jnp.concatenate([pair[lo][1], pair[lo + 1][1]], axis=0).astype(BF16)

    for idx in range(2 * n_chunk):
        off_re = (idx // n_chunk) * (2 * S5_HALF_STATE) + (idx % n_chunk) * LANE
        scan_chunk(off_re, off_re + S5_HALF_STATE)

    if need_out:
        for hf in range(2):
            y_half = _dot(h_sc[:, hf * 2 * S5_HALF_STATE:(hf + 1) * 2 * S5_HALF_STATE], c_ref[hf])
            for k in range(half_u // LANE):
                y_sc[hf * (half_u // LANE) + k] = y_half[:, k * LANE:(k + 1) * LANE]
        for k in range(n_u):
            for s in range(S5_SEG):
                y_ref[s * S5_STEPS:(s + 1) * S5_STEPS, k * LANE:(k + 1) * LANE] = (
                    y_sc[k, pl.ds(s, S5_STEPS, stride=S5_SEG), :])

    @pl.when(blk == pl.num_programs(1) - 1)
    def _():
        hf_ref[...] = carry_sc[...]


def _s5_scan(small, b_blk, c_blk, tabs, h0, *, batch, direction, need_out):
    n = small.shape[0]
    length = n // batch
    tb = S5_BLOCK
    n_blocks = length // tb
    reverse = direction == 1
    if reverse:
        row = lambda b, i: b * n_blocks + (n_blocks - 1 - i)
    else:
        row = lambda b, i: b * n_blocks + i
    const2 = lambda shape: pl.BlockSpec(shape, lambda b, i: (0,) * len(shape))
    n_u = S5_WIDTH // LANE
    u_spec = lambda k: pl.BlockSpec((tb, LANE), lambda b, i: (row(b, i), BD_PAD // LANE + k))
    in_specs = [u_spec(k) for k in range(n_u)] + [const2(b_blk.shape)]
    args = [small] * n_u + [b_blk]
    if need_out:
        in_specs.append(const2(c_blk.shape))
        args.append(c_blk)
    in_specs += [const2(t.shape) for t in tabs]
    in_specs.append(pl.BlockSpec((None, 8, S5_LANES), lambda b, i: (b, 0, 0)))
    args += list(tabs) + [h0]
    carry_spec = pl.BlockSpec((None, 8, S5_LANES), lambda b, i: (b, 0, 0))
    carry_shape = jax.ShapeDtypeStruct((batch, 8, S5_LANES), F32)
    scratch = [pltpu.VMEM((tb, S5_WIDTH), F32), pltpu.VMEM((tb, S5_LANES), F32)]
    if need_out:
        out_specs = [pl.BlockSpec((tb, S5_WIDTH), lambda b, i: (row(b, i), 0)), carry_spec]
        out_shape = [jax.ShapeDtypeStruct((n, S5_WIDTH), F32), carry_shape]
        scratch += [pltpu.VMEM((tb, S5_LANES), BF16), pltpu.VMEM((n_u, tb, LANE), F32)]
    else:
        out_specs = [carry_spec]
        out_shape = [carry_shape]
    scratch.append(pltpu.VMEM((8, S5_LANES), F32))
    res = pl.pallas_call(
        functools.partial(_s5_kernel, reverse=reverse, need_out=need_out),
        grid=(batch, n_blocks),
        in_specs=in_specs,
        out_specs=out_specs,
        out_shape=out_shape,
        scratch_shapes=scratch,
        compiler_params=pltpu.CompilerParams(dimension_semantics=("arbitrary", "arbitrary"),
                                             vmem_limit_bytes=VMEM_LIMIT),
        name=f"s5_d{direction}" + ("_out" if need_out else "_state"),
    )(*args)
    return res if need_out else (None, res[0])


def _out_kernel(x_ref, gate_ref, of_ref, ob_ref, dnw_ref, szdn_ref, sgdn_ref, sgs5_ref, szs5_ref,
                yf_ref, yb_ref, small_ref, d_ref, gluw_ref, glub_ref, wpd_ref, wps_ref, wo_ref, fnw_ref,
                o_ref):
    o = of_ref[...] + ob_ref[...]
    parts = []
    for h in range(DN_HEADS):
        t = o[:, h * DN_DV:(h + 1) * DN_DV]
        parts.append(t * lax.rsqrt(jnp.mean(t * t, axis=-1, keepdims=True) + NORM_EPS))
    y_dn = jnp.concatenate(parts, axis=1) * dnw_ref[...] * szdn_ref[...]
    u = small_ref[:, BD_PAD:BD_PAD + S5_WIDTH]
    y = d_ref[...] * u + yf_ref[...] + yb_ref[...]
    y = 0.5 * y * (1.0 + jnp.tanh(0.7978845608028654 * (y + 0.044715 * (y * y * y))))
    y = y * _sigmoid(_dot(y.astype(BF16), gluw_ref[...]) + glub_ref[...])
    y_s5 = y * szs5_ref[...]
    merged = (sgdn_ref[...] * _dot(y_dn.astype(BF16), wpd_ref[...])
              + sgs5_ref[...] * _dot(y_s5.astype(BF16), wps_ref[...]))
    out = _dot(merged.astype(BF16), wo_ref[...])
    xo = x_ref[...] + gate_ref[...] * out
    o_ref[...] = xo * lax.rsqrt(jnp.mean(xo * xo, axis=-1, keepdims=True) + NORM_EPS) * fnw_ref[...]


def _output_stage(x2, mod3, o_f, o_b, dn_norm_w, gates, y_f, y_b, small, s5_d, glu_w, glu_b,
                  w_proj_dn, w_proj_s5, w_out, final_norm_w, *, batch, tm):
    n = x2.shape[0]
    tiles_per_batch = n // batch // tm
    tok = lambda width, col: pl.BlockSpec((tm, width), lambda i: (i, col))
    const = lambda shape: pl.BlockSpec(shape, lambda i: (0, 0))
    in_specs = [
        tok(D_MODEL, 0),
        pl.BlockSpec((None, 1, D_MODEL), lambda i: (i // tiles_per_batch, 0, 2)),
        tok(DN_WIDTH, 0), tok(DN_WIDTH, 0), const((1, DN_WIDTH)),
        tok(D_MODEL, 0), tok(D_MODEL, 1), tok(D_MODEL, 2), tok(S5_WIDTH, 3 * D_MODEL // S5_WIDTH),
        tok(S5_WIDTH, 0), tok(S5_WIDTH, 0), tok(SMALL_W, 0), const((1, S5_WIDTH)),
        const((S5_WIDTH, S5_WIDTH)), const((1, S5_WIDTH)),
        const((DN_WIDTH, D_MODEL)), const((S5_WIDTH, D_MODEL)), const((D_MODEL, D_MODEL)),
        const((1, D_MODEL)),
    ]
    return pl.pallas_call(
        _out_kernel,
        grid=(n // tm,),
        in_specs=in_specs,
        out_specs=pl.BlockSpec((tm, D_MODEL), lambda i: (i, 0)),
        out_shape=jax.ShapeDtypeStruct((n, D_MODEL), F32),
        compiler_params=pltpu.CompilerParams(dimension_semantics=("arbitrary",),
                                             vmem_limit_bytes=VMEM_LIMIT),
        name="output_stage",
    )(x2, mod3, o_f, o_b, dn_norm_w, gates, gates, gates, gates, y_f, y_b, small, s5_d,
      glu_w, glu_b, w_proj_dn, w_proj_s5, w_out, final_norm_w)


def _gate_lane_params(a_log, dt_bias):
    par = jnp.zeros((8, LANE), F32)
    par = par.at[0, 2 * DN_HEADS:4 * DN_HEADS].set(a_log.reshape(-1).astype(F32))
    par = par.at[1, 2 * DN_HEADS:4 * DN_HEADS].set(dt_bias.reshape(-1).astype(F32))
    return par


def _expand_matrix(first_lane):
    lane = jnp.arange(LANE)[:, None]
    head = jnp.arange(DN_WIDTH)[None, :] // DN_DV
    return (lane == first_lane + head).astype(BF16)


def kernel(x, c, ctx, c_ctx, w_ada, b_ada, norm_w, w_in, conv_w, dn_A_log, dn_dt_bias, dn_norm_w,
           s5_lam_re, s5_lam_im, s5_log_step, s5_B_re, s5_B_im, s5_C_re, s5_C_im, s5_D, glu_w, glu_b,
           w_proj_dn, w_proj_s5, w_out, final_norm_w):
    batch, seq, d_model = x.shape
    ctx_len = ctx.shape[1]
    depth = w_in.shape[0]
    assert d_model == D_MODEL and depth == 1 and batch <= 4
    assert seq % 1024 == 0 and ctx_len % S5_BLOCK == 0 and ctx_len & (ctx_len - 1) == 0
    i = 0
    x2 = x.reshape(batch * seq, D_MODEL)
    ctx2 = ctx.reshape(batch * ctx_len, D_MODEL)

    w_full = w_in[i]
    w_qkv = w_full[:, :QKV_W].astype(BF16)
    w_small = jnp.concatenate(
        [w_full[:, O_BETA:O_U], jnp.zeros((D_MODEL, BD_PAD - (O_U - O_BETA)), w_full.dtype),
         w_full[:, O_U:STATE_COLS]], axis=1).astype(BF16)
    w_gate = jnp.concatenate(
        [w_full[:, O_ZDN:O_ZS5], w_full[:, O_GDN:O_GS5], w_full[:, O_GS5:IN_WIDTH],
         w_full[:, O_ZS5:O_GDN]], axis=1).astype(BF16)
    conv_w8 = jnp.concatenate([conv_w[i], jnp.zeros((8 - CONV_K, QKV_W), conv_w.dtype)], axis=0).astype(F32)
    nw = norm_w[i].reshape(1, D_MODEL).astype(F32)
    e_beta = [_expand_matrix(DN_HEADS * d) for d in range(2)]
    e_gate = [_expand_matrix(2 * DN_HEADS + DN_HEADS * d) for d in range(2)]
    par = _gate_lane_params(dn_A_log[i], dn_dt_bias[i])

    cvec = jnp.zeros((8, D_MODEL), F32).at[:batch].set(c.astype(F32)).at[batch].set(c_ctx.astype(F32))
    mod = _mod_vectors(cvec, w_ada[i].astype(F32), b_ada[i].reshape(1, -1).astype(F32))
    mod3 = mod.reshape(8, 1, 3 * D_MODEL)

    pow_re, pow_im, bbar_re, bbar_im = _s5_prepare(
        s5_lam_re[i], s5_lam_im[i], s5_log_step[i], s5_B_re[i], s5_B_im[i])
    b_blk, c_blk = _s5_block_matrices(
        bbar_re, bbar_im,
        s5_C_re[i].astype(F32), s5_C_im[i].astype(F32))
    pad8 = lambda t: jnp.concatenate([t, jnp.zeros((-t.shape[0] % 8,) + t.shape[1:], t.dtype)], axis=0)
    s5_tabs = [tuple(pad8(_s5_lane_layout(t, t)[:, d]) for t in (pow_re, pow_im)) for d in range(2)]

    ctx_row = lambda t: batch
    ctx_qkv = _inproj(ctx2, nw, mod3, ctx_row, w_qkv, conv_w8, epilogue="qkv", rows=ctx_len, tm=ctx_len)
    ctx_small = _inproj(ctx2, nw, mod3, ctx_row, w_small, None, epilogue="raw", rows=ctx_len, tm=ctx_len)
    zero_state = jnp.zeros((2, batch, DN_HEADS, DN_DK, DN_DV), F32)
    zero_h = jnp.zeros((batch, 8, S5_LANES), F32)
    _, _, s_ctx = _delta_rule(ctx_qkv, ctx_small, par, e_beta, e_gate, zero_state, batch=batch, need_out=False)
    h_ctx = []
    for d in range(2):
        _, h_fin = _s5_scan(ctx_small, b_blk[d], None, s5_tabs[d], zero_h,
                            batch=batch, direction=d, need_out=False)
        h_ctx.append(h_fin)

    tm = 1024
    tiles_per_batch = seq // tm
    lat_row = lambda t: t // tiles_per_batch
    qkv = _inproj(x2, nw, mod3, lat_row, w_qkv, conv_w8, epilogue="qkv", rows=GRID_W, tm=tm)
    small = _inproj(x2, nw, mod3, lat_row, w_small, None, epilogue="raw", rows=GRID_W, tm=tm)
    gates = _inproj(x2, nw, mod3, lat_row, w_gate, None, epilogue="gate", rows=GRID_W, tm=tm)
    o_f, o_b, _ = _delta_rule(qkv, small, par, e_beta, e_gate, s_ctx, batch=batch, need_out=True)
    o_dir = [o_f, o_b]
    y_dir = []
    for d in range(2):
        y_d, _ = _s5_scan(small, b_blk[d], c_blk[d], s5_tabs[d], h_ctx[d],
                          batch=batch, direction=d, need_out=True)
        y_dir.append(y_d)

    out = _output_stage(
        x2, mod3, o_dir[0], o_dir[1], jnp.tile(dn_norm_w[i].astype(F32), DN_HEADS).reshape(1, DN_WIDTH),
        gates, y_dir[0], y_dir[1], small, s5_D[i].reshape(1, S5_WIDTH).astype(F32),
        glu_w[i].astype(BF16), glu_b[i].reshape(1, S5_WIDTH).astype(F32),
        w_proj_dn[i].astype(BF16), w_proj_s5[i].astype(BF16), w_out[i].astype(BF16),
        final_norm_w.reshape(1, D_MODEL).astype(F32), batch=batch, tm=256)
    return out.reshape(batch, seq, D_MODEL)
```

```python
import functools

import jax
import jax.numpy as jnp
from jax import lax
from jax.experimental import pallas as pl
from jax.experimental.pallas import tpu as pltpu

F32 = jnp.float32
BF16 = jnp.bfloat16
HIGHEST = lax.Precision.HIGHEST

D_MODEL = 1024
NORM_EPS = 1e-6
GRID_W = 64
DN_HEADS = 8
DN_DK = 128
DN_DV = 128
DN_WIDTH = DN_HEADS * DN_DV
CONV_K = 5
CHUNK = 64
S5_WIDTH = 512
S5_GROUP = 16
S5_GROUPS = S5_WIDTH // S5_GROUP
S5_STATE = 64
S5_HALF_GROUPS = S5_GROUPS // 2
S5_HALF_STATE = S5_HALF_GROUPS * S5_STATE
S5_LANES = 4 * S5_HALF_STATE
S5_BLOCK = 256
S5_SEG = 8
S5_STEPS = S5_BLOCK // S5_SEG
S5_SEG_LOG = 3
QK_W = DN_HEADS * DN_DK
QKV_W = 2 * QK_W + DN_WIDTH
O_BETA = QKV_W
O_DECAY = O_BETA + 2 * DN_HEADS
O_U = O_DECAY + 2 * DN_HEADS
STATE_COLS = O_U + S5_WIDTH
O_ZDN = STATE_COLS
O_ZS5 = O_ZDN + DN_WIDTH
O_GDN = O_ZS5 + S5_WIDTH
O_GS5 = O_GDN + D_MODEL
IN_WIDTH = O_GS5 + D_MODEL

LANE = 128
BD_PAD = LANE
SMALL_W = BD_PAD + S5_WIDTH
GATE_W = 3 * D_MODEL + S5_WIDTH
COL_TILE = 512
VMEM_LIMIT = 48 * 1024 * 1024


def _sigmoid(x):
    return 0.5 * jnp.tanh(0.5 * x) + 0.5


def _silu(x):
    half = 0.5 * x
    return half * jnp.tanh(half) + half


def _softplus(x):
    return jnp.maximum(x, 0.0) + jnp.log1p(jnp.exp(-jnp.abs(x)))


def _dot(a, b):
    return jnp.dot(a, b, preferred_element_type=F32)


def _dot_hi(a, b):
    return jnp.dot(a, b, precision=HIGHEST, preferred_element_type=F32)


def _dot_nt(a, b):
    return lax.dot_general(a, b, (((1,), (1,)), ((), ())), preferred_element_type=F32)


def _dot_tn(a, b):
    return lax.dot_general(a, b, (((0,), (0,)), ((), ())), preferred_element_type=F32)


def _split3(x):
    hi = x.astype(BF16)
    r1 = x - hi.astype(F32)
    mid = r1.astype(BF16)
    lo = (r1 - mid.astype(F32)).astype(BF16)
    return hi, mid, lo


def _dot_sel_rhs(x, sel):
    hi, mid, lo = _split3(x)
    return _dot(hi, sel) + _dot(mid, sel) + _dot(lo, sel)


def _dot_sel_lhs(sel, x):
    hi, mid, lo = _split3(x)
    return _dot(sel, hi) + _dot(sel, mid) + _dot(sel, lo)


def _mod_kernel(c_ref, w_ref, b_ref, o_ref):
    c = c_ref[...]
    s = _silu(c)
    o_ref[...] = _dot_hi(s, w_ref[...]) + b_ref[...]


def _mod_vectors(cvec, w_ada, b_ada):
    n = w_ada.shape[1]
    return pl.pallas_call(
        _mod_kernel,
        grid=(n // COL_TILE,),
        in_specs=[
            pl.BlockSpec((8, D_MODEL), lambda j: (0, 0)),
            pl.BlockSpec((D_MODEL, COL_TILE), lambda j: (0, j)),
            pl.BlockSpec((1, COL_TILE), lambda j: (0, j)),
        ],
        out_specs=pl.BlockSpec((8, COL_TILE), lambda j: (0, j)),
        out_shape=jax.ShapeDtypeStruct((8, n), F32),
        compiler_params=pltpu.CompilerParams(dimension_semantics=("arbitrary",)),
        name="mod_vectors",
    )(cvec, w_ada, b_ada)


def _s5_prep_kernel(lre_ref, lim_ref, ls_ref, bre_ref, bim_ref, pr_ref, pi_ref, bbr_ref, bbi_ref):
    lam_re = lre_ref[...]
    lam_im = lim_ref[...]
    dt = jnp.exp(ls_ref[...])
    mag = jnp.exp(lam_re * dt)
    abar_re = mag * jnp.cos(lam_im * dt)
    abar_im = mag * jnp.sin(lam_im * dt)
    num_re = abar_re - 1.0
    num_im = abar_im
    den = lam_re * lam_re + lam_im * lam_im
    f_re = (num_re * lam_re + num_im * lam_im) / den
    f_im = (num_im * lam_re - num_re * lam_im) / den
    b_re = bre_ref[...]
    b_im = bim_ref[...]
    bbr_ref[...] = f_re * b_re - f_im * b_im
    bbi_ref[...] = f_re * b_im + f_im * b_re
    pr_ref[0] = abar_re
    pi_ref[0] = abar_im
    pr, pi = abar_re, abar_im
    n = 1
    while n < S5_STEPS:
        pr, pi = pr * pr - pi * pi, 2.0 * pr * pi
        n *= 2
    for k in range(S5_SEG_LOG):
        pr_ref[1 + k] = pr
        pi_ref[1 + k] = pi
        pr, pi = pr * pr - pi * pi, 2.0 * pr * pi


def _s5_prepare(lam_re, lam_im, log_step, b_re, b_im):
    dg = 2 * S5_GROUPS
    lre = lam_re.reshape(dg, 1, S5_STATE)
    lim = lam_im.reshape(dg, 1, S5_STATE)
    ls = jnp.broadcast_to(log_step.reshape(dg, 1, 1), (dg, 1, S5_STATE))
    bre = jnp.swapaxes(b_re.reshape(dg, S5_STATE, S5_GROUP), 1, 2)
    bim = jnp.swapaxes(b_im.reshape(dg, S5_STATE, S5_GROUP), 1, 2)
    full3 = lambda shape: pl.BlockSpec(shape, lambda: (0,) * len(shape))
    pow_shape = (1 + S5_SEG_LOG, dg, 1, S5_STATE)
    pr, pi, bbr, bbi = pl.pallas_call(
        _s5_prep_kernel,
        in_specs=[full3(lre.shape), full3(lim.shape), full3(ls.shape), full3(bre.shape), full3(bim.shape)],
        out_specs=[full3(pow_shape), full3(pow_shape), full3(bre.shape), full3(bim.shape)],
        out_shape=[jax.ShapeDtypeStruct(pow_shape, F32), jax.ShapeDtypeStruct(pow_shape, F32),
                   jax.ShapeDtypeStruct(bre.shape, F32), jax.ShapeDtypeStruct(bim.shape, F32)],
        name="s5_discretise",
    )(lre.astype(F32), lim.astype(F32), ls.astype(F32), bre.astype(F32), bim.astype(F32))
    per_dir = lambda t: t.reshape(t.shape[0], 2, S5_GROUPS, S5_STATE)
    bbar_re = bbr.reshape(2, S5_GROUPS, S5_GROUP, S5_STATE)
    bbar_im = bbi.reshape(2, S5_GROUPS, S5_GROUP, S5_STATE)
    return per_dir(pr), per_dir(pi), bbar_re, bbar_im


def _s5_lane_layout(re, im):
    lead = re.shape[:-2]
    re = re.reshape(*lead, 2, S5_HALF_STATE)
    im = im.reshape(*lead, 2, S5_HALF_STATE)
    return jnp.concatenate([re, im], axis=-1).reshape(*lead, S5_LANES)


def _s5_block_matrices(bbar_re, bbar_im, c_re, c_im):
    eye = jnp.eye(S5_HALF_GROUPS, dtype=F32)

    def in_map(bb):
        bb = bb.reshape(2, 2, S5_HALF_GROUPS, S5_GROUP, S5_STATE)
        m = jnp.einsum('dhgcp,gk->dhgckp', bb, eye)
        return m.reshape(2, 2, S5_HALF_GROUPS * S5_GROUP, S5_HALF_STATE)

    def out_map(cc):
        cc = cc.reshape(2, 2, S5_HALF_GROUPS, S5_GROUP, S5_STATE)
        m = jnp.einsum('dhgcp,gk->dhgpkc', cc, eye)
        return m.reshape(2, 2, S5_HALF_STATE, S5_HALF_GROUPS * S5_GROUP)

    b_blk = jnp.concatenate([in_map(bbar_re), in_map(bbar_im)], axis=-1)
    c_blk = jnp.concatenate([out_map(c_re), out_map(-c_im)], axis=-2)
    return b_blk.astype(BF16), c_blk.astype(BF16)


def _conv_silu(acc, cw, rows):
    tm = acc.shape[0]
    pos = lax.broadcasted_iota(jnp.int32, acc.shape, 0) & (rows - 1)
    half = CONV_K // 2
    y = acc * cw[half:half + 1, :]
    for j in range(CONV_K):
        s = j - half
        if s == 0:
            continue
        shifted = pltpu.roll(acc, (-s) % tm, axis=0)
        valid = (pos >= -s) if s < 0 else (pos < rows - s)
        y = y + jnp.where(valid, shifted, 0.0) * cw[j:j + 1, :]
    return _silu(y)


def _l2norm_heads(y, scale):
    outs = []
    for h in range(y.shape[1] // DN_DK):
        t = y[:, h * DN_DK:(h + 1) * DN_DK]
        outs.append(t * (lax.rsqrt(jnp.sum(t * t, axis=-1, keepdims=True) + NORM_EPS) * scale))
    return jnp.concatenate(outs, axis=1)


def _inproj_kernel(*refs, epilogue, rows):
    if epilogue == "qkv":
        x_ref, nw_ref, sc_ref, sh_ref, w_ref, cw_ref, o_ref, h_sc = refs
    else:
        x_ref, nw_ref, sc_ref, sh_ref, w_ref, o_ref, h_sc = refs
    j = pl.program_id(1)

    @pl.when(j == 0)
    def _():
        x = x_ref[...]
        y = x * lax.rsqrt(jnp.mean(x * x, axis=-1, keepdims=True) + NORM_EPS) * nw_ref[...]
        h_sc[...] = (y * (1.0 + sc_ref[...]) + sh_ref[...]).astype(BF16)

    acc = _dot(h_sc[...], w_ref[...])
    if epilogue == "raw":
        o_ref[...] = acc
    elif epilogue == "gate":
        is_silu = jnp.logical_or(j < 2, j == 6)

        @pl.when(is_silu)
        def _():
            o_ref[...] = _silu(acc).astype(o_ref.dtype)

        @pl.when(jnp.logical_not(is_silu))
        def _():
            o_ref[...] = _sigmoid(acc).astype(o_ref.dtype)
    else:
        y = _conv_silu(acc, cw_ref[...], rows)
        n_qk_tiles = QK_W // COL_TILE

        @pl.when(j < 2 * n_qk_tiles)
        def _():
            scale = jnp.where(j < n_qk_tiles, DN_DK ** -0.5, 1.0)
            o_ref[...] = _l2norm_heads(y, scale)

        @pl.when(j >= 2 * n_qk_tiles)
        def _():
            o_ref[...] = y


def _inproj(x2, norm_w, mod3, mod_row, w, conv_w, *, epilogue, rows, tm):
    n = x2.shape[0]
    width = w.shape[1]
    tn = COL_TILE if width % COL_TILE == 0 else width
    grid = (n // tm, width // tn)
    w_tiles = jnp.swapaxes(w.reshape(D_MODEL, width // tn, tn), 0, 1)
    vec = lambda k: pl.BlockSpec((None, 1, D_MODEL), lambda i, j: (mod_row(i), 0, k))
    in_specs = [
        pl.BlockSpec((tm, D_MODEL), lambda i, j: (i, 0)),
        pl.BlockSpec((1, D_MODEL), lambda i, j: (0, 0)),
        vec(1),
        vec(0),
        pl.BlockSpec((None, D_MODEL, tn), lambda i, j: (j, 0, 0)),
    ]
    args = [x2, norm_w, mod3, mod3, w_tiles]
    if epilogue == "qkv":
        in_specs.append(pl.BlockSpec((8, tn), lambda i, j: (0, j)))
        args.append(conv_w)
    return pl.pallas_call(
        functools.partial(_inproj_kernel, epilogue=epilogue, rows=rows),
        grid=grid,
        in_specs=in_specs,
        out_specs=pl.BlockSpec((tm, tn), lambda i, j: (i, j)),
        out_shape=jax.ShapeDtypeStruct((n, width), BF16 if epilogue == "gate" else F32),
        scratch_shapes=[pltpu.VMEM((tm, D_MODEL), BF16)],
        compiler_params=pltpu.CompilerParams(dimension_semantics=("arbitrary", "arbitrary"),
                                             vmem_limit_bytes=VMEM_LIMIT),
        name="inproj_" + epilogue,
    )(*args)


def _block_masks(n, reverse):
    ri = lax.broadcasted_iota(jnp.int32, (n, n), 0)
    ci = lax.broadcasted_iota(jnp.int32, (n, n), 1)
    lo, hi = (ri, ci) if reverse else (ci, ri)
    masks = []
    b = 1
    while b < n:
        masks.append(((hi ^ lo) < 2 * b) & ((hi & b) != 0) & ((lo & b) == 0))
        b *= 2
    return masks


def _delta_chunk_prep(q, k, v, bd, par, e_beta, e_gate, direction):
    reverse = direction == 1
    beta_all = _sigmoid(bd)
    g_all = -jnp.exp(par[0:1, :]) * _softplus(bd + par[1:2, :])
    ri = lax.broadcasted_iota(jnp.int32, (CHUNK, CHUNK), 0)
    ci = lax.broadcasted_iota(jnp.int32, (CHUNK, CHUNK), 1)
    incl = (ri <= ci) if reverse else (ri >= ci)
    strict = (ri < ci) if reverse else (ri > ci)
    tri = jnp.where(incl, 1.0, 0.0).astype(BF16)

    g_cum = _dot_sel_lhs(tri, g_all)
    g_cum_t = jnp.transpose(jnp.concatenate([g_cum, jnp.zeros_like(g_cum)], axis=0))
    beta_x = _dot(beta_all.astype(BF16), e_beta)
    g_hi = g_cum.astype(BF16)
    g_x = _dot(g_hi, e_gate) + _dot((g_cum - g_hi.astype(F32)).astype(BF16), e_gate)
    edge = 0 if reverse else CHUNK - 1
    g_last = g_x[edge:edge + 1, :]
    e_g = jnp.exp(g_x)
    a_end = jnp.exp(g_last)
    kb = k * beta_x
    k_bf = k.astype(BF16)
    kq = jnp.concatenate([kb, q], axis=0).astype(BF16)
    vb = (v * beta_x).astype(BF16)
    kbe = (kb * e_g).astype(BF16)
    qd = (q * e_g).astype(BF16)
    kend = (k * jnp.exp(g_last - g_x)).astype(BF16)

    heads = []
    for h in range(DN_HEADS):
        sl = slice(h * DN_DK, (h + 1) * DN_DK)
        gate_lane = 2 * DN_HEADS + DN_HEADS * direction + h
        diff = g_x[:, h * DN_DK:h * DN_DK + CHUNK] - g_cum_t[gate_lane:gate_lane + 1, 0:CHUNK]
        heads.append(dict(
            dec=jnp.exp(jnp.where(incl, diff, -jnp.inf)),
            kq=kq[:, sl], k=k_bf[:, sl],
            rhs=jnp.concatenate([vb[:, sl], kbe[:, sl]], axis=1),
            qd=qd[:, sl], kend=kend[:, sl], a_end=a_end[:, sl]))
    return heads, strict, _block_masks(CHUNK, reverse)


def _delta_kernel(*refs, need_out):
    ins, rest = refs[:14], refs[14:]
    (qf_ref, kf_ref, vf_ref, bdf_ref, qb_ref, kb_ref, vb_ref, bdb_ref,
     par_ref, eb0_ref, eg0_ref, eb1_ref, eg1_ref, s0_ref) = ins
    if need_out:
        of_ref, ob_ref, sf_ref, s_sc = rest
    else:
        sf_ref, s_sc = rest
    c = pl.program_id(0)
    batch = qf_ref.shape[0]

    @pl.when(c == 0)
    def _():
        s_sc[...] = s0_ref[...]

    par = par_ref[...]
    units = []
    for b in range(batch):
        for d, (q_ref, k_ref, v_ref, bd_ref, eb_ref, eg_ref) in enumerate(
                ((qf_ref, kf_ref, vf_ref, bdf_ref, eb0_ref, eg0_ref),
                 (qb_ref, kb_ref, vb_ref, bdb_ref, eb1_ref, eg1_ref))):
            heads, strict, masks = _delta_chunk_prep(q_ref[b], k_ref[b], v_ref[b], bd_ref[b], par,
                                                     eb_ref[...], eg_ref[...], d)
            for h, unit in enumerate(heads):
                unit.update(b=b, d=d, h=h, strict=strict, masks=masks)
                units.append(unit)

    ri = lax.broadcasted_iota(jnp.int32, (CHUNK, CHUNK), 0)
    ci = lax.broadcasted_iota(jnp.int32, (CHUNK, CHUNK), 1)
    eye = jnp.where(ri == ci, 1.0, 0.0).astype(F32)
    for u in units:
        aq = _dot_nt(u["kq"], u["k"])
        u["am"] = jnp.where(u["strict"], aq[:CHUNK] * u["dec"], 0.0)
        u["attn"] = (aq[CHUNK:] * u["dec"]).astype(BF16)
        u["inv"] = eye - jnp.where(u["masks"][0], u["am"], 0.0)
    for level in range(1, len(units[0]["masks"])):
        for u in units:
            e = jnp.where(u["masks"][level], u["am"], 0.0).astype(BF16)
            u["inv_bf"] = u["inv"].astype(BF16)
            u["t"] = _dot(e, u["inv_bf"]).astype(BF16)
        for u in units:
            u["inv"] = u["inv"] - _dot(u["inv_bf"], u["t"])
    for u in units:
        u["uw"] = _dot(u["inv"].astype(BF16), u["rhs"])
    for u in units:
        u["s"] = s_sc[u["d"], u["b"], u["h"]]
        lhs = jnp.concatenate([u["uw"][:, DN_DV:].astype(BF16), u["qd"]], axis=0)
        u["ws"] = _dot(lhs, u["s"].astype(BF16))
    outs = {}
    for u in units:
        v_new = (u["uw"][:, :DN_DV] - u["ws"][:CHUNK]).astype(BF16)
        if need_out:
            outs.setdefault((u["d"], u["b"]), []).append(u["ws"][CHUNK:] + _dot(u["attn"], v_new))
        s_sc[u["d"], u["b"], u["h"]] = u["s"] * u["a_end"] + _dot_tn(u["kend"], v_new)

    if need_out:
        for b in range(batch):
            of_ref[b] = jnp.concatenate(outs[(0, b)], axis=1)
            ob_ref[b] = jnp.concatenate(outs[(1, b)], axis=1)

    @pl.when(c == pl.num_programs(0) - 1)
    def _():
        sf_ref[...] = s_sc[...]


def _delta_rule(qkv, small, par, e_beta, e_gate, s0, *, batch, need_out):
    n = qkv.shape[0]
    length = n // batch
    n_chunks = length // CHUNK
    qkv3 = qkv.reshape(batch, length, QKV_W)
    small3 = small.reshape(batch, length, SMALL_W)
    fwd = lambda c: c
    bwd = lambda c: n_chunks - 1 - c
    in_specs = []
    for row in (fwd, bwd):
        in_specs += [
            pl.BlockSpec((batch, CHUNK, QK_W), lambda c, row=row: (0, row(c), 0)),
            pl.BlockSpec((batch, CHUNK, QK_W), lambda c, row=row: (0, row(c), 1)),
            pl.BlockSpec((batch, CHUNK, DN_WIDTH), lambda c, row=row: (0, row(c), 2)),
            pl.BlockSpec((batch, CHUNK, BD_PAD), lambda c, row=row: (0, row(c), 0)),
        ]
    const = lambda shape: pl.BlockSpec(shape, lambda c: (0,) * len(shape))
    state_dims = (2, batch, DN_HEADS, DN_DK, DN_DV)
    in_specs += [const((8, LANE)), const((LANE, DN_WIDTH)), const((LANE, DN_WIDTH)),
                 const((LANE, DN_WIDTH)), const((LANE, DN_WIDTH)), const(state_dims)]
    state_shape = jax.ShapeDtypeStruct(state_dims, F32)
    if need_out:
        out_specs = [pl.BlockSpec((batch, CHUNK, DN_WIDTH), lambda c: (0, fwd(c), 0)),
                     pl.BlockSpec((batch, CHUNK, DN_WIDTH), lambda c: (0, bwd(c), 0)), const(state_dims)]
        out_shape = [jax.ShapeDtypeStruct((batch, length, DN_WIDTH), F32)] * 2 + [state_shape]
    else:
        out_specs = [const(state_dims)]
        out_shape = [state_shape]
    res = pl.pallas_call(
        functools.partial(_delta_kernel, need_out=need_out),
        grid=(n_chunks,),
        in_specs=in_specs,
        out_specs=out_specs,
        out_shape=out_shape,
        scratch_shapes=[pltpu.VMEM(state_dims, F32)],
        compiler_params=pltpu.CompilerParams(dimension_semantics=("arbitrary",),
                                             vmem_limit_bytes=VMEM_LIMIT),
        name="delta_out" if need_out else "delta_state",
    )(qkv3, qkv3, qkv3, small3, qkv3, qkv3, qkv3, small3, par,
      e_beta[0], e_gate[0], e_beta[1], e_gate[1], s0)
    if not need_out:
        return None, None, res[0]
    return res[0].reshape(n, DN_WIDTH), res[1].reshape(n, DN_WIDTH), res[2]


def _cmul_add(a_re, a_im, h_re, h_im, x_re, x_im):
    return a_re * h_re - a_im * h_im + x_re, a_re * h_im + a_im * h_re + x_im


def _s5_kernel(*refs, reverse, need_out):
    n_u = S5_WIDTH // LANE
    u_refs, refs = refs[:n_u], refs[n_u:]
    if need_out:
        b_ref, c_ref, pr_ref, pi_ref, h0_ref, y_ref, hf_ref, up_sc, x_sc, h_sc, y_sc, carry_sc = refs
    else:
        b_ref, pr_ref, pi_ref, h0_ref, hf_ref, up_sc, x_sc, carry_sc = refs
    blk = pl.program_id(1)

    @pl.when(blk == 0)
    def _():
        carry_sc[...] = h0_ref[...]

    for j in range(S5_STEPS):
        for k in range(n_u):
            up_sc[j * S5_SEG:(j + 1) * S5_SEG, k * LANE:(k + 1) * LANE] = (
                u_refs[k][pl.ds(j, S5_SEG, stride=S5_STEPS), :])
    u = up_sc[...].astype(BF16)
    half_u = S5_WIDTH // 2
    for hf in range(2):
        x_sc[:, hf * 2 * S5_HALF_STATE:(hf + 1) * 2 * S5_HALF_STATE] = _dot(
            u[:, hf * half_u:(hf + 1) * half_u], b_ref[hf])

    order = list(range(S5_STEPS - 1, -1, -1)) if reverse else list(range(S5_STEPS))
    seg = lax.broadcasted_iota(jnp.int32, (S5_SEG, LANE), 0)
    first_seg = S5_SEG - 1 if reverse else 0
    last_seg = 0 if reverse else S5_SEG - 1
    n_chunk = S5_HALF_STATE // LANE

    def scan_chunk(off_re, off_im):
        re_l, im_l = pl.ds(off_re, LANE), pl.ds(off_im, LANE)
        a_re, a_im = pr_ref[0:1, re_l], pi_ref[0:1, re_l]
        h_re = jnp.zeros((S5_SEG, LANE), F32)
        h_im = jnp.zeros((S5_SEG, LANE), F32)
        for j in order:
            rows = pl.ds(j * S5_SEG, S5_SEG)
            h_re, h_im = _cmul_add(a_re, a_im, h_re, h_im, x_sc[rows, re_l], x_sc[rows, im_l])
        toward = 1 if not reverse else S5_SEG - 1
        f_re = jnp.where(seg == first_seg, carry_sc[0:1, re_l], pltpu.roll(h_re, toward, axis=0))
        f_im = jnp.where(seg == first_seg, carry_sc[0:1, im_l], pltpu.roll(h_im, toward, axis=0))
        for k in range(S5_SEG_LOG):
            d = 1 << k
            valid = (seg < S5_SEG - d) if reverse else (seg >= d)
            amount = S5_SEG - d if reverse else d
            p_re = jnp.where(valid, pltpu.roll(f_re, amount, axis=0), 0.0)
            p_im = jnp.where(valid, pltpu.roll(f_im, amount, axis=0), 0.0)
            f_re, f_im = _cmul_add(pr_ref[1 + k:2 + k, re_l], pi_ref[1 + k:2 + k, re_l], p_re, p_im, f_re, f_im)
        n_re, n_im = _cmul_add(pr_ref[1:2, re_l], pi_ref[1:2, re_l], f_re, f_im, h_re, h_im)
        carry_sc[0:1, re_l] = n_re[last_seg:last_seg + 1, :]
        carry_sc[0:1, im_l] = n_im[last_seg:last_seg + 1, :]
        if need_out:
            h_re, h_im = f_re, f_im
            for j0 in range(0, S5_STEPS, 2):
                pair = {}
                for j in order[j0:j0 + 2]:
                    rows = pl.ds(j * S5_SEG, S5_SEG)
                    h_re, h_im = _cmul_add(a_re, a_im, h_re, h_im, x_sc[rows, re_l], x_sc[rows, im_l])
                    pair[j] = (h_re, h_im)
                lo = min(pair)
                rows2 = pl.ds(lo * S5_SEG, 2 * S5_SEG)
                h_sc[rows2, re_l] = jnp.concatenate([pair[lo][0], pair[lo + 1][0]], axis=0).astype(BF16)
                h_sc[rows2, im_l] = jnp.concatenate([pair[lo][1], pair[lo + 1][1]], axis=0).astype(BF16)

    for idx in range(2 * n_chunk):
        off_re = (idx // n_chunk) * (2 * S5_HALF_STATE) + (idx % n_chunk) * LANE
        scan_chunk(off_re, off_re + S5_HALF_STATE)

    if need_out:
        for hf in range(2):
            y_half = _dot(h_sc[:, hf * 2 * S5_HALF_STATE:(hf + 1) * 2 * S5_HALF_STATE], c_ref[hf])
            for k in range(half_u // LANE):
                y_sc[hf * (half_u // LANE) + k] = y_half[:, k * LANE:(k + 1) * LANE]
        for k in range(n_u):
            for s in range(S5_SEG):
                y_ref[s * S5_STEPS:(s + 1) * S5_STEPS, k * LANE:(k + 1) * LANE] = (
                    y_sc[k, pl.ds(s, S5_STEPS, stride=S5_SEG), :])

    @pl.when(blk == pl.num_programs(1) - 1)
    def _():
        hf_ref[...] = carry_sc[...]


def _s5_scan(small, b_blk, c_blk, tabs, h0, *, batch, direction, need_out):
    n = small.shape[0]
    length = n // batch
    tb = S5_BLOCK
    n_blocks = length // tb
    reverse = direction == 1
    if reverse:
        row = lambda b, i: b * n_blocks + (n_blocks - 1 - i)
    else:
        row = lambda b, i: b * n_blocks + i
    const2 = lambda shape: pl.BlockSpec(shape, lambda b, i: (0,) * len(shape))
    n_u = S5_WIDTH // LANE
    u_spec = lambda k: pl.BlockSpec((tb, LANE), lambda b, i: (row(b, i), BD_PAD // LANE + k))
    in_specs = [u_spec(k) for k in range(n_u)] + [const2(b_blk.shape)]
    args = [small] * n_u + [b_blk]
    if need_out:
        in_specs.append(const2(c_blk.shape))
        args.append(c_blk)
    in_specs += [const2(t.shape) for t in tabs]
    in_specs.append(pl.BlockSpec((None, 8, S5_LANES), lambda b, i: (b, 0, 0)))
    args += list(tabs) + [h0]
    carry_spec = pl.BlockSpec((None, 8, S5_LANES), lambda b, i: (b, 0, 0))
    carry_shape = jax.ShapeDtypeStruct((batch, 8, S5_LANES), F32)
    scratch = [pltpu.VMEM((tb, S5_WIDTH), F32), pltpu.VMEM((tb, S5_LANES), F32)]
    if need_out:
        out_specs = [pl.BlockSpec((tb, S5_WIDTH), lambda b, i: (row(b, i), 0)), carry_spec]
        out_shape = [jax.ShapeDtypeStruct((n, S5_WIDTH), F32), carry_shape]
        scratch += [pltpu.VMEM((tb, S5_LANES), BF16), pltpu.VMEM((n_u, tb, LANE), F32)]
    else:
        out_specs = [carry_spec]
        out_shape = [carry_shape]
    scratch.append(pltpu.VMEM((8, S5_LANES), F32))
    res = pl.pallas_call(
        functools.partial(_s5_kernel, reverse=reverse, need_out=need_out),
        grid=(batch, n_blocks),
        in_specs=in_specs,
        out_specs=out_specs,
        out_shape=out_shape,
        scratch_shapes=scratch,
        compiler_params=pltpu.CompilerParams(dimension_semantics=("arbitrary", "arbitrary"),
                                             vmem_limit_bytes=VMEM_LIMIT),
        name=f"s5_d{direction}" + ("_out" if need_out else "_state"),
    )(*args)
    return res if need_out else (None, res[0])


def _out_kernel(x_ref, gate_ref, of_ref, ob_ref, dnw_ref, szdn_ref, sgdn_ref, sgs5_ref, szs5_ref,
                yf_ref, yb_ref, small_ref, d_ref, gluw_ref, glub_ref, wpd_ref, wps_ref, wo_ref, fnw_ref,
                o_ref):
    o = of_ref[...] + ob_ref[...]
    parts = []
    for h in range(DN_HEADS):
        t = o[:, h * DN_DV:(h + 1) * DN_DV]
        parts.append(t * lax.rsqrt(jnp.mean(t * t, axis=-1, keepdims=True) + NORM_EPS))
    y_dn = jnp.concatenate(parts, axis=1) * dnw_ref[...] * szdn_ref[...]
    u = small_ref[:, BD_PAD:BD_PAD + S5_WIDTH]
    y = d_ref[...] * u + yf_ref[...] + yb_ref[...]
    y = 0.5 * y * (1.0 + jnp.tanh(0.7978845608028654 * (y + 0.044715 * (y * y * y))))
    y = y * _sigmoid(_dot(y.astype(BF16), gluw_ref[...]) + glub_ref[...])
    y_s5 = y * szs5_ref[...]
    merged = (sgdn_ref[...] * _dot(y_dn.astype(BF16), wpd_ref[...])
              + sgs5_ref[...] * _dot(y_s5.astype(BF16), wps_ref[...]))
    out = _dot(merged.astype(BF16), wo_ref[...])
    xo = x_ref[...] + gate_ref[...] * out
    o_ref[...] = xo * lax.rsqrt(jnp.mean(xo * xo, axis=-1, keepdims=True) + NORM_EPS) * fnw_ref[...]


def _output_stage(x2, mod3, o_f, o_b, dn_norm_w, gates, y_f, y_b, small, s5_d, glu_w, glu_b,
                  w_proj_dn, w_proj_s5, w_out, final_norm_w, *, batch, tm):
    n = x2.shape[0]
    tiles_per_batch = n // batch // tm
    tok = lambda width, col: pl.BlockSpec((tm, width), lambda i: (i, col))
    const = lambda shape: pl.BlockSpec(shape, lambda i: (0, 0), pipeline_mode=pl.Buffered(1))
    in_specs = [
        tok(D_MODEL, 0),
        pl.BlockSpec((None, 1, D_MODEL), lambda i: (i // tiles_per_batch, 0, 2)),
        tok(DN_WIDTH, 0), tok(DN_WIDTH, 0), const((1, DN_WIDTH)),
        tok(D_MODEL, 0), tok(D_MODEL, 1), tok(D_MODEL, 2), tok(S5_WIDTH, 3 * D_MODEL // S5_WIDTH),
        tok(S5_WIDTH, 0), tok(S5_WIDTH, 0), tok(SMALL_W, 0), const((1, S5_WIDTH)),
        const((S5_WIDTH, S5_WIDTH)), const((1, S5_WIDTH)),
        const((DN_WIDTH, D_MODEL)), const((S5_WIDTH, D_MODEL)), const((D_MODEL, D_MODEL)),
        const((1, D_MODEL)),
    ]
    return pl.pallas_call(
        _out_kernel,
        grid=(n // tm,),
        in_specs=in_specs,
        out_specs=pl.BlockSpec((tm, D_MODEL), lambda i: (i, 0)),
        out_shape=jax.ShapeDtypeStruct((n, D_MODEL), F32),
        compiler_params=pltpu.CompilerParams(dimension_semantics=("arbitrary",),
                                             vmem_limit_bytes=VMEM_LIMIT),
        name="output_stage",
    )(x2, mod3, o_f, o_b, dn_norm_w, gates, gates, gates, gates, y_f, y_b, small, s5_d,
      glu_w, glu_b, w_proj_dn, w_proj_s5, w_out, final_norm_w)


def _gate_lane_params(a_log, dt_bias):
    par = jnp.zeros((8, LANE), F32)
    par = par.at[0, 2 * DN_HEADS:4 * DN_HEADS].set(a_log.reshape(-1).astype(F32))
    par = par.at[1, 2 * DN_HEADS:4 * DN_HEADS].set(dt_bias.reshape(-1).astype(F32))
    return par


def _expand_matrix(first_lane):
    lane = jnp.arange(LANE)[:, None]
    head = jnp.arange(DN_WIDTH)[None, :] // DN_DV
    return (lane == first_lane + head).astype(BF16)


def kernel(x, c, ctx, c_ctx, w_ada, b_ada, norm_w, w_in, conv_w, dn_A_log, dn_dt_bias, dn_norm_w,
           s5_lam_re, s5_lam_im, s5_log_step, s5_B_re, s5_B_im, s5_C_re, s5_C_im, s5_D, glu_w, glu_b,
           w_proj_dn, w_proj_s5, w_out, final_norm_w):
    batch, seq, d_model = x.shape
    ctx_len = ctx.shape[1]
    depth = w_in.shape[0]
    assert d_model == D_MODEL and depth == 1 and batch <= 4
    assert seq % 1024 == 0 and ctx_len % S5_BLOCK == 0 and ctx_len & (ctx_len - 1) == 0
    i = 0
    x2 = x.reshape(batch * seq, D_MODEL)
    ctx2 = ctx.reshape(batch * ctx_len, D_MODEL)

    w_full = w_in[i]
    w_qkv = w_full[:, :QKV_W].astype(BF16)
    w_small = jnp.concatenate(
        [w_full[:, O_BETA:O_U], jnp.zeros((D_MODEL, BD_PAD - (O_U - O_BETA)), w_full.dtype),
         w_full[:, O_U:STATE_COLS]], axis=1).astype(BF16)
    w_gate = jnp.concatenate(
        [w_full[:, O_ZDN:O_ZS5], w_full[:, O_GDN:O_GS5], w_full[:, O_GS5:IN_WIDTH],
         w_full[:, O_ZS5:O_GDN]], axis=1).astype(BF16)
    conv_w8 = jnp.concatenate([conv_w[i], jnp.zeros((8 - CONV_K, QKV_W), conv_w.dtype)], axis=0).astype(F32)
    nw = norm_w[i].reshape(1, D_MODEL).astype(F32)
    e_beta = [_expand_matrix(DN_HEADS * d) for d in range(2)]
    e_gate = [_expand_matrix(2 * DN_HEADS + DN_HEADS * d) for d in range(2)]
    par = _gate_lane_params(dn_A_log[i], dn_dt_bias[i])

    cvec = jnp.zeros((8, D_MODEL), F32).at[:batch].set(c.astype(F32)).at[batch].set(c_ctx.astype(F32))
    mod = _mod_vectors(cvec, w_ada[i].astype(F32), b_ada[i].reshape(1, -1).astype(F32))
    mod3 = mod.reshape(8, 1, 3 * D_MODEL)

    pow_re, pow_im, bbar_re, bbar_im = _s5_prepare(
        s5_lam_re[i], s5_lam_im[i], s5_log_step[i], s5_B_re[i], s5_B_im[i])
    b_blk, c_blk = _s5_block_matrices(
        bbar_re, bbar_im,
        s5_C_re[i].astype(F32), s5_C_im[i].astype(F32))
    pad8 = lambda t: jnp.concatenate([t, jnp.zeros((-t.shape[0] % 8,) + t.shape[1:], t.dtype)], axis=0)
    s5_tabs = [tuple(pad8(_s5_lane_layout(t, t)[:, d]) for t in (pow_re, pow_im)) for d in range(2)]

    ctx_row = lambda t: batch
    ctx_qkv = _inproj(ctx2, nw, mod3, ctx_row, w_qkv, conv_w8, epilogue="qkv", rows=ctx_len, tm=ctx_len)
    ctx_small = _inproj(ctx2, nw, mod3, ctx_row, w_small, None, epilogue="raw", rows=ctx_len, tm=ctx_len)
    zero_state = jnp.zeros((2, batch, DN_HEADS, DN_DK, DN_DV), F32)
    zero_h = jnp.zeros((batch, 8, S5_LANES), F32)
    _, _, s_ctx = _delta_rule(ctx_qkv, ctx_small, par, e_beta, e_gate, zero_state, batch=batch, need_out=False)
    h_ctx = []
    for d in range(2):
        _, h_fin = _s5_scan(ctx_small, b_blk[d], None, s5_tabs[d], zero_h,
                            batch=batch, direction=d, need_out=False)
        h_ctx.append(h_fin)

    tm = 1024
    tiles_per_batch = seq // tm
    lat_row = lambda t: t // tiles_per_batch
    qkv = _inproj(x2, nw, mod3, lat_row, w_qkv, conv_w8, epilogue="qkv", rows=GRID_W, tm=tm)
    small = _inproj(x2, nw, mod3, lat_row, w_small, None, epilogue="raw", rows=GRID_W, tm=tm)
    gates = _inproj(x2, nw, mod3, lat_row, w_gate, None, epilogue="gate", rows=GRID_W, tm=tm)
    o_f, o_b, _ = _delta_rule(qkv, small, par, e_beta, e_gate, s_ctx, batch=batch, need_out=True)
    o_dir = [o_f, o_b]
    y_dir = []
    for d in range(2):
        y_d, _ = _s5_scan(small, b_blk[d], c_blk[d], s5_tabs[d], h_ctx[d],
                          batch=batch, direction=d, need_out=True)
        y_dir.append(y_d)

    out = _output_stage(
        x2, mod3, o_dir[0], o_dir[1], jnp.tile(dn_norm_w[i].astype(F32), DN_HEADS).reshape(1, DN_WIDTH),
        gates, y_dir[0], y_dir[1], small, s5_D[i].reshape(1, S5_WIDTH).astype(F32),
        glu_w[i].astype(BF16), glu_b[i].reshape(1, S5_WIDTH).astype(F32),
        w_proj_dn[i].astype(BF16), w_proj_s5[i].astype(BF16), w_out[i].astype(BF16),
        final_norm_w.reshape(1, D_MODEL).astype(F32), batch=batch, tm=512)
    return out.reshape(batch, seq, D_MODEL)
```
